```python
import jax, jax.numpy as jnp
from jax import lax
import numpy as np

D_MODEL = 2048
BATCH = 4
SEQ = 2048
DEPTH = 1
DEC_BATCH = 128
DEC_SEQ = 4
PAST_LEN = 16384
PAGE_SIZE = 128

SB_HEADS = 8
SB_KV_HEADS = 2
SB_GROUP = SB_HEADS // SB_KV_HEADS
SB_HEAD_DIM = 128
SB_SCALE = SB_HEAD_DIM ** -0.5
MLA_HEADS = 8
Q_LORA_RANK = 512
KV_LORA_RANK = 512
QK_NOPE_DIM = 128
QK_ROPE_DIM = 64
V_HEAD_DIM = 128
MLA_SCALE = (QK_NOPE_DIM + QK_ROPE_DIM) ** -0.5
ROPE_THETA = 10000.0
N_MEM = 256
MEM_HEADS = 4
MEM_HEAD_DIM = 256
MEM_SCALE = MEM_HEAD_DIM ** -0.5
N_GROUPS = 8
EXPERTS_PER_GROUP = 8
N_EXPERTS = N_GROUPS * EXPERTS_PER_GROUP
TOP_K = 2
D_EXPERT = 512
MOE_BLOCK = 128
Q_BLOCK = 128
NORM_EPS = 1e-6
N_BRANCHES = 3
IN_SPLITS = (SB_HEADS * SB_HEAD_DIM, SB_KV_HEADS * SB_HEAD_DIM, SB_KV_HEADS * SB_HEAD_DIM,
             Q_LORA_RANK, KV_LORA_RANK, QK_ROPE_DIM, MEM_HEADS * MEM_HEAD_DIM, N_BRANCHES * D_MODEL)
IN_COLS = sum(IN_SPLITS)
F32 = jnp.float32

kernel_name = "hybrid_stickbreak_mla_memory_hmoe_step"


def rmsnorm(x, g):
    xf = x.astype(F32)
    y = xf * lax.rsqrt(jnp.mean(xf * xf, axis=-1, keepdims=True) + NORM_EPS)
    return (y * g.astype(F32)).astype(x.dtype)


def apply_rope(x, pos):
    half = x.shape[-1] // 2
    inv = ROPE_THETA ** (-jnp.arange(half, dtype=F32) / half)
    ang = pos.astype(F32)[:, None] * inv[None, :]
    ang = ang.reshape((ang.shape[0],) + (1,) * (x.ndim - 3) + (half,))
    cos, sin = jnp.cos(ang), jnp.sin(ang)
    x1 = x[..., :half].astype(F32)
    x2 = x[..., half:].astype(F32)
    return jnp.concatenate([x1 * cos - x2 * sin, x1 * sin + x2 * cos], axis=-1).astype(x.dtype)


def mixer_inputs(x, pos, g_attn, w_in, g_q, w_uq, g_kv, w_uk):
    b, t, _ = x.shape
    h = rmsnorm(x, g_attn)
    offs = np.cumsum(IN_SPLITS)[:-1].tolist()
    sb_q, sb_k, sb_v, cq, ckv, kr, mem_q, gates = jnp.split(h @ w_in, offs, axis=-1)
    sb_q = sb_q.reshape(b, t, SB_HEADS, SB_HEAD_DIM)
    sb_k = sb_k.reshape(b, t, SB_KV_HEADS, SB_HEAD_DIM)
    sb_v = sb_v.reshape(b, t, SB_KV_HEADS, SB_HEAD_DIM)
    q = jnp.einsum('btc,chr->bthr', rmsnorm(cq, g_q), w_uq)
    q_rope = apply_rope(q[..., QK_NOPE_DIM:], pos)
    q_lat = jnp.einsum('bthn,chn->bthc', q[..., :QK_NOPE_DIM], w_uk)
    ckv = rmsnorm(ckv, g_kv)
    kr = apply_rope(kr, pos)
    mem_q = mem_q.reshape(b, t, MEM_HEADS, MEM_HEAD_DIM)
    return sb_q, sb_k, sb_v, q_lat, q_rope, ckv, kr, mem_q, gates


def sb_weights(z, mask, carry_fail):
    log_fail = jax.nn.log_sigmoid(-z)
    if mask is not None:
        log_fail = jnp.where(mask, log_fail, 0.0)
    after = lax.cumsum(log_fail, axis=z.ndim - 1, reverse=True) - log_fail
    if carry_fail is not None:
        after = after + carry_fail[..., None]
    w = jnp.exp(jax.nn.log_sigmoid(z) + after)
    if mask is not None:
        w = jnp.where(mask, w, 0.0)
    return w, log_fail.sum(-1)


def sb_prompt(q, k, v):
    b, s, _, _ = q.shape
    kpos = jnp.arange(s)
    v32 = v.astype(F32)

    def block(i):
        start = i * Q_BLOCK
        qb = lax.dynamic_slice_in_dim(q, start, Q_BLOCK, axis=1)
        qb = qb.reshape(b, Q_BLOCK, SB_KV_HEADS, SB_GROUP, SB_HEAD_DIM)
        z = jnp.einsum('bqkgd,bskd->bkgqs', qb, k, preferred_element_type=F32) * SB_SCALE
        mask = kpos[None, :] < (start + jnp.arange(Q_BLOCK))[:, None]
        w, _ = sb_weights(z, mask, None)
        return jnp.einsum('bkgqs,bskd->bqkgd', w, v32)

    out = lax.map(block, jnp.arange(s // Q_BLOCK))
    return jnp.moveaxis(out, 0, 1).reshape(b, s, SB_HEADS * SB_HEAD_DIM)


def sb_sample(q, k_new, v_new, cache_k, cache_v, layer, page_table):
    b, t, _, _ = q.shape
    qg = q.reshape(b, t, SB_KV_HEADS, SB_GROUP, SB_HEAD_DIM)
    qpos = jnp.arange(t)
    z = jnp.einsum('bqkgd,bskd->bkgqs', qg, k_new, preferred_element_type=F32) * SB_SCALE
    w, fail = sb_weights(z, qpos[None, :] < qpos[:, None], None)
    acc = jnp.einsum('bkgqs,bskd->bkgqd', w, v_new.astype(F32))

    def step(carry, pages):
        acc, fail = carry
        kp = cache_k[layer, pages]
        vp = cache_v[layer, pages]
        z = jnp.einsum('bqkgd,bskd->bkgqs', qg, kp, preferred_element_type=F32) * SB_SCALE
        w, page_fail = sb_weights(z, None, fail)
        acc = acc + jnp.einsum('bkgqs,bskd->bkgqd', w, vp.astype(F32))
        return (acc, fail + page_fail), None

    (acc, _), _ = lax.scan(step, (acc, fail), page_table[:, ::-1].T)
    return acc.transpose(0, 3, 1, 2, 4).reshape(b, t, SB_HEADS * SB_HEAD_DIM)


def mla_scores(q_lat, q_rope, ckv, kr):
    return (jnp.einsum('bqhc,bsc->bhqs', q_lat, ckv, preferred_element_type=F32)
            + jnp.einsum('bqhr,bsr->bhqs', q_rope, kr, preferred_element_type=F32)) * MLA_SCALE


def mla_prompt(q_lat, q_rope, ckv, kr):
    b, s, _, _ = q_lat.shape
    kpos = jnp.arange(s)
    ckv32 = ckv.astype(F32)

    def block(i):
        start = i * Q_BLOCK
        ql = lax.dynamic_slice_in_dim(q_lat, start, Q_BLOCK, axis=1)
        qr = lax.dynamic_slice_in_dim(q_rope, start, Q_BLOCK, axis=1)
        sc = mla_scores(ql, qr, ckv, kr)
        mask = kpos[None, :] <= (start + jnp.arange(Q_BLOCK))[:, None]
        p = jax.nn.softmax(jnp.where(mask, sc, -jnp.inf), axis=-1)
        return jnp.einsum('bhqs,bsc->bqhc', p, ckv32)

    out = lax.map(block, jnp.arange(s // Q_BLOCK))
    return jnp.moveaxis(out, 0, 1).reshape(b, s, MLA_HEADS, KV_LORA_RANK)


def mla_sample(q_lat, q_rope, ckv_new, kr_new, cache_ckv, cache_kr, layer, page_table):
    b, t, _, _ = q_lat.shape
    qpos = jnp.arange(t)
    s = jnp.where(qpos[None, :] <= qpos[:, None], mla_scores(q_lat, q_rope, ckv_new, kr_new), -jnp.inf)
    m = s.max(-1)
    p = jnp.exp(s - m[..., None])
    l = p.sum(-1)
    acc = jnp.einsum('bhqs,bsc->bhqc', p, ckv_new.astype(F32))

    def step(carry, pages):
        m, l, acc = carry
        cp = cache_ckv[layer, pages]
        s = mla_scores(q_lat, q_rope, cp, cache_kr[layer, pages])
        m_new = jnp.maximum(m, s.max(-1))
        alpha = jnp.exp(m - m_new)
        p = jnp.exp(s - m_new[..., None])
        acc = acc * alpha[..., None] + jnp.einsum('bhqs,bsc->bhqc', p, cp.astype(F32))
        return (m_new, l * alpha + p.sum(-1), acc), None

    (m, l, acc), _ = lax.scan(step, (m, l, acc), page_table.T)
    return (acc / l[..., None]).transpose(0, 2, 1, 3)


def mla_out(lat, w_uv):
    b, t, _, _ = lat.shape
    o = jnp.einsum('bthc,chv->bthv', lat.astype(w_uv.dtype), w_uv)
    return o.reshape(b, t, MLA_HEADS * V_HEAD_DIM)


def memory_kv(mem, g_mem, w_mem_kv):
    b, m, _ = mem.shape
    k, v = jnp.split(rmsnorm(mem, g_mem) @ w_mem_kv, 2, axis=-1)
    return (k.reshape(b, m, MEM_HEADS, MEM_HEAD_DIM), v.reshape(b, m, MEM_HEADS, MEM_HEAD_DIM))


def memory_attend(q, k, v):
    b, t, _, _ = q.shape
    s = jnp.einsum('bthd,bmhd->bhtm', q, k, preferred_element_type=F32) * MEM_SCALE
    p = jax.nn.softmax(s, axis=-1)
    o = jnp.einsum('bhtm,bmhd->bthd', p, v.astype(F32))
    return o.reshape(b, t, MEM_HEADS * MEM_HEAD_DIM)


def hier_moe(h, w_grp, b_grp, w_rtr, b_rtr, w_gate, w_up, w_down):
    T, D = h.shape
    grp_logits = (h @ w_grp).astype(F32) + b_grp.astype(F32)
    g_idx = jnp.argmax(grp_logits, axis=-1)
    g_p = jnp.take_along_axis(jax.nn.softmax(grp_logits, axis=-1), g_idx[:, None], axis=-1)[:, 0]
    e_logits = ((h @ w_rtr).astype(F32) + b_rtr.astype(F32)).reshape(T, N_GROUPS, EXPERTS_PER_GROUP)
    in_grp = jnp.take_along_axis(e_logits, g_idx[:, None, None], axis=1)[:, 0]
    top_val, top_loc = lax.top_k(in_grp, TOP_K)
    top_w = jax.nn.softmax(top_val, axis=-1) * g_p[:, None]
    top_e = g_idx[:, None] * EXPERTS_PER_GROUP + top_loc

    tk = T * TOP_K
    n_blocks = -(-tk // MOE_BLOCK) + N_EXPERTS
    flat_e = top_e.reshape(-1).astype(jnp.int32)
    flat_t = jnp.repeat(jnp.arange(T, dtype=jnp.int32), TOP_K)
    order = jnp.argsort(flat_e)
    se, st, sw = flat_e[order], flat_t[order], top_w.reshape(-1)[order]
    counts = jnp.bincount(flat_e, length=N_EXPERTS)
    start = jnp.cumsum(counts) - counts
    padded = (counts + MOE_BLOCK - 1) // MOE_BLOCK * MOE_BLOCK
    pad_end = jnp.cumsum(padded)
    pad_start = pad_end - padded
    dest = pad_start[se] + jnp.arange(tk) - start[se]
    slot_tok = jnp.full((n_blocks * MOE_BLOCK,), T, jnp.int32).at[dest].set(st)
    slot_w = jnp.zeros((n_blocks * MOE_BLOCK,), F32).at[dest].set(sw)
    block_e = jnp.minimum(jnp.searchsorted(pad_end, jnp.arange(n_blocks) * MOE_BLOCK, side='right'),
                          N_EXPERTS - 1)
    h_pad = jnp.concatenate([h, jnp.zeros((1, D), h.dtype)], axis=0)

    def run_block(args):
        tok, wt, e = args
        xb = h_pad[tok]
        a = jax.nn.silu(xb @ w_gate[e]) * (xb @ w_up[e])
        return (a @ w_down[e]) * wt[:, None].astype(h.dtype)

    y = lax.map(run_block, (slot_tok.reshape(n_blocks, MOE_BLOCK),
                            slot_w.reshape(n_blocks, MOE_BLOCK), block_e))
    out = jnp.zeros((T + 1, D), y.dtype).at[slot_tok].add(y.reshape(-1, D))
    return out[:T]


def block_tail(x, o_sb, o_mla, o_mem, gates, w_br_sb, w_br_mla, w_br_mem, w_out,
               g_ffn, w_grp, b_grp, w_rtr, b_rtr, w_gate, w_up, w_down):
    dt = x.dtype
    g = jax.nn.sigmoid(gates.astype(F32)).astype(dt)
    g_sb, g_mla, g_mem = jnp.split(g, N_BRANCHES, axis=-1)
    mix = (g_sb * (o_sb.astype(dt) @ w_br_sb) + g_mla * (o_mla.astype(dt) @ w_br_mla)
           + g_mem * (o_mem.astype(dt) @ w_br_mem))
    x = x + mix @ w_out
    b, t, d = x.shape
    y = hier_moe(rmsnorm(x, g_ffn).reshape(b * t, d), w_grp, b_grp, w_rtr, b_rtr, w_gate, w_up, w_down)
    return x + y.reshape(b, t, d).astype(dt)


def setup_inputs(seed: int = 0) -> dict:
    key = jax.random.key(seed)
    keys = list(jax.random.split(key, 48))

    def nrm(shape, scale=1.0):
        return jax.random.normal(keys.pop(), shape, F32) * scale

    n_pages = PAST_LEN // PAGE_SIZE
    n_pool = DEC_BATCH * n_pages * 5 // 4
    page_table = jax.random.permutation(keys.pop(), n_pool)[:DEC_BATCH * n_pages]
    page_table = page_table.reshape(DEC_BATCH, n_pages).astype(jnp.int32)
    D = D_MODEL
    return {
        'x_prompt': nrm((BATCH, SEQ, D)),
        'x_sample': nrm((DEC_BATCH, DEC_SEQ, D)),
        'mem_prompt': nrm((BATCH, N_MEM, D)),
        'cache_sb_k': nrm((DEPTH, n_pool, PAGE_SIZE, SB_KV_HEADS, SB_HEAD_DIM)),
        'cache_sb_v': nrm((DEPTH, n_pool, PAGE_SIZE, SB_KV_HEADS, SB_HEAD_DIM)),
        'cache_mla_ckv': nrm((DEPTH, n_pool, PAGE_SIZE, KV_LORA_RANK)),
        'cache_mla_krope': nrm((DEPTH, n_pool, PAGE_SIZE, QK_ROPE_DIM)),
        'cache_mem_k': nrm((DEPTH, DEC_BATCH, N_MEM, MEM_HEADS, MEM_HEAD_DIM)),
        'cache_mem_v': nrm((DEPTH, DEC_BATCH, N_MEM, MEM_HEADS, MEM_HEAD_DIM)),
        'page_table': page_table,
        'g_attn_norm': 1.0 + nrm((DEPTH, D), 0.01),
        'w_in': nrm((DEPTH, D, IN_COLS), D ** -0.5),
        'g_q_norm': 1.0 + nrm((DEPTH, Q_LORA_RANK), 0.01),
        'w_uq': nrm((DEPTH, Q_LORA_RANK, MLA_HEADS, QK_NOPE_DIM + QK_ROPE_DIM), Q_LORA_RANK ** -0.5),
        'g_kv_norm': 1.0 + nrm((DEPTH, KV_LORA_RANK), 0.01),
        'w_uk': nrm((DEPTH, KV_LORA_RANK, MLA_HEADS, QK_NOPE_DIM), KV_LORA_RANK ** -0.5),
        'w_uv': nrm((DEPTH, KV_LORA_RANK, MLA_HEADS, V_HEAD_DIM), KV_LORA_RANK ** -0.5),
        'g_mem_norm': 1.0 + nrm((DEPTH, D), 0.01),
        'w_mem_kv': nrm((DEPTH, D, 2 * MEM_HEADS * MEM_HEAD_DIM), D ** -0.5),
        'w_br_sb': nrm((DEPTH, SB_HEADS * SB_HEAD_DIM, D), (SB_HEADS * SB_HEAD_DIM) ** -0.5),
        'w_br_mla': nrm((DEPTH, MLA_HEADS * V_HEAD_DIM, D), (MLA_HEADS * V_HEAD_DIM) ** -0.5),
        'w_br_mem': nrm((DEPTH, MEM_HEADS * MEM_HEAD_DIM, D), (MEM_HEADS * MEM_HEAD_DIM) ** -0.5),
        'w_out': nrm((DEPTH, D, D), D ** -0.5),
        'g_ffn_norm': 1.0 + nrm((DEPTH, D), 0.01),
        'w_grp': nrm((DEPTH, D, N_GROUPS), D ** -0.5),
        'b_grp': nrm((DEPTH, N_GROUPS), 0.01),
        'w_rtr': nrm((DEPTH, D, N_EXPERTS), D ** -0.5),
        'b_rtr': nrm((DEPTH, N_EXPERTS), 0.01),
        'w_gate': nrm((DEPTH, N_EXPERTS, D, D_EXPERT), D ** -0.5),
        'w_up': nrm((DEPTH, N_EXPERTS, D, D_EXPERT), D ** -0.5),
        'w_down': nrm((DEPTH, N_EXPERTS, D_EXPERT, D), D_EXPERT ** -0.5),
        'g_final': 1.0 + nrm((D,), 0.01),
    }


def reference(x_prompt, x_sample, mem_prompt, cache_sb_k, cache_sb_v, cache_mla_ckv, cache_mla_krope,
              cache_mem_k, cache_mem_v, page_table, g_attn_norm, w_in, g_q_norm, w_uq, g_kv_norm,
              w_uk, w_uv, g_mem_norm, w_mem_kv, w_br_sb, w_br_mla, w_br_mem, w_out, g_ffn_norm,
              w_grp, b_grp, w_rtr, b_rtr, w_gate, w_up, w_down, g_final):
    past_len = page_table.shape[1] * PAGE_SIZE
    pos_p = jnp.arange(x_prompt.shape[1])
    pos_s = past_len + jnp.arange(x_sample.shape[1])
    xp, xs = x_prompt, x_sample
    p_sbk, p_sbv, p_ckv, p_kr, p_mk, p_mv = [], [], [], [], [], []
    s_sbk, s_sbv, s_ckv, s_kr = [], [], [], []
    for l in range(DEPTH):
        sq, sk, sv, ql, qr, ckv, kr, mq, gates = mixer_inputs(
            xp, pos_p, g_attn_norm[l], w_in[l], g_q_norm[l], w_uq[l], g_kv_norm[l], w_uk[l])
        mk, mv = memory_kv(mem_prompt, g_mem_norm[l], w_mem_kv[l])
        o_sb = sb_prompt(sq, sk, sv)
        o_mla = mla_out(mla_prompt(ql, qr, ckv, kr), w_uv[l])
        o_mem = memory_attend(mq, mk, mv)
        xp = block_tail(xp, o_sb, o_mla, o_mem, gates, w_br_sb[l], w_br_mla[l], w_br_mem[l], w_out[l],
                        g_ffn_norm[l], w_grp[l], b_grp[l], w_rtr[l], b_rtr[l], w_gate[l], w_up[l], w_down[l])
        p_sbk.append(sk); p_sbv.append(sv); p_ckv.append(ckv); p_kr.append(kr)
        p_mk.append(mk); p_mv.append(mv)
        sq, sk, sv, ql, qr, ckv, kr, mq, gates = mixer_inputs(
            xs, pos_s, g_attn_norm[l], w_in[l], g_q_norm[l], w_uq[l], g_kv_norm[l], w_uk[l])
        o_sb = sb_sample(sq, sk, sv, cache_sb_k, cache_sb_v, l, page_table)
        o_mla = mla_out(mla_sample(ql, qr, ckv, kr, cache_mla_ckv, cache_mla_krope, l, page_table), w_uv[l])
        o_mem = memory_attend(mq, cache_mem_k[l], cache_mem_v[l])
        xs = block_tail(xs, o_sb, o_mla, o_mem, gates, w_br_sb[l], w_br_mla[l], w_br_mem[l], w_out[l],
                        g_ffn_norm[l], w_grp[l], b_grp[l], w_rtr[l], b_rtr[l], w_gate[l], w_up[l], w_down[l])
        s_sbk.append(sk); s_sbv.append(sv); s_ckv.append(ckv); s_kr.append(kr)
    y_prompt = rmsnorm(xp, g_final)
    y_sample = rmsnorm(xs, g_final)
    return (y_prompt, y_sample, jnp.stack(p_sbk), jnp.stack(p_sbv), jnp.stack(p_ckv), jnp.stack(p_kr),
            jnp.stack(p_mk), jnp.stack(p_mv), jnp.stack(s_sbk), jnp.stack(s_sbv), jnp.stack(s_ckv),
            jnp.stack(s_kr))
```

```python
import functools

import jax
import jax.numpy as jnp
from jax import lax
from jax.experimental import pallas as pl
from jax.experimental.pallas import tpu as pltpu

F32 = jnp.float32
BF16 = jnp.bfloat16
I32 = jnp.int32

NORM_EPS = 1e-6
ROPE_THETA = 10000.0
PAGE_SIZE = 128
SB_HEADS, SB_KV_HEADS, SB_HEAD_DIM = 8, 2, 128
SB_GROUP = SB_HEADS // SB_KV_HEADS
SB_SCALE = SB_HEAD_DIM ** -0.5
MLA_HEADS, Q_LORA, KV_LORA, NOPE, ROPE, V_DIM = 8, 512, 512, 128, 64, 128
MLA_SCALE = (NOPE + ROPE) ** -0.5
QK_CAT = KV_LORA + 128
MEM_HEADS, MEM_DIM = 4, 256
MEM_SCALE = MEM_DIM ** -0.5
N_GROUPS, EPG, N_EXPERTS, D_EXPERT = 8, 8, 64, 512
MOE_BLOCK = 128
NEG = -1e30
VMEM_LIMIT = 56 * 1024 * 1024
NT = (((1,), (1,)), ((), ()))


def _params(sem):
    return pltpu.CompilerParams(dimension_semantics=sem, vmem_limit_bytes=VMEM_LIMIT)


def _rms(x, g):
    return x * lax.rsqrt(jnp.mean(x * x, axis=-1, keepdims=True) + NORM_EPS) * g


def _rmsnorm_kernel(x_ref, g_ref, o_ref):
    o_ref[...] = _rms(x_ref[...], g_ref[...]).astype(o_ref.dtype)


def rmsnorm(x, g, out_dtype, tm):
    r, d = x.shape
    return pl.pallas_call(
        _rmsnorm_kernel, grid=(r // tm,),
        in_specs=[pl.BlockSpec((tm, d), lambda i: (i, 0)), pl.BlockSpec((1, d), lambda i: (0, 0))],
        out_specs=pl.BlockSpec((tm, d), lambda i: (i, 0)),
        out_shape=jax.ShapeDtypeStruct((r, d), out_dtype),
        compiler_params=_params(("parallel",)))(x, g.reshape(1, d))


def _mm_kernel(a_ref, b_ref, o_ref):
    o_ref[...] = jnp.dot(a_ref[...], b_ref[...], preferred_element_type=F32).astype(o_ref.dtype)


def matmul(a, b, out_dtype, tm, tn):
    m, k = a.shape
    n = b.shape[1]
    return pl.pallas_call(
        _mm_kernel, grid=(m // tm, n // tn),
        in_specs=[pl.BlockSpec((tm, k), lambda i, j: (i, 0)), pl.BlockSpec((k, tn), lambda i, j: (0, j))],
        out_specs=pl.BlockSpec((tm, tn), lambda i, j: (i, j)),
        out_shape=jax.ShapeDtypeStruct((m, n), out_dtype),
        compiler_params=_params(("parallel", "arbitrary")))(a, b)


def _mla_q_kernel(cq_ref, g_ref, wuq_ref, wuk_ref, sel_ref, cos_ref, sin_ref, o_ref):
    cqn = _rms(cq_ref[...], g_ref[...]).astype(BF16)
    q = jnp.dot(cqn, wuq_ref[...], preferred_element_type=F32) * MLA_SCALE
    n_np = MLA_HEADS * NOPE
    half = MLA_HEADS * (ROPE // 2)
    r1 = q[:, n_np:n_np + half]
    r2 = q[:, n_np + half:n_np + 2 * half]
    cos, sin = cos_ref[...], sin_ref[...]
    rot = jnp.concatenate([r1 * cos - r2 * sin, r1 * sin + r2 * cos], axis=-1).astype(BF16)
    tail = jnp.dot(rot, sel_ref[...], preferred_element_type=F32)
    for h in range(MLA_HEADS):
        nope = q[:, h * NOPE:(h + 1) * NOPE].astype(BF16)
        lat = jnp.dot(nope, wuk_ref[h], preferred_element_type=F32)
        o_ref[:, h * QK_CAT:h * QK_CAT + KV_LORA] = lat.astype(o_ref.dtype)
        o_ref[:, h * QK_CAT + KV_LORA:(h + 1) * QK_CAT] = tail[:, h * 128:(h + 1) * 128].astype(o_ref.dtype)


def mla_q(hmat, cq_col, g_q, wuq, wuk_t, sel, cos_q, sin_q, tm):
    t = hmat.shape[0]
    return pl.pallas_call(
        _mla_q_kernel, grid=(t // tm,),
        in_specs=[pl.BlockSpec((tm, Q_LORA), lambda i: (i, cq_col)),
                  pl.BlockSpec((1, Q_LORA), lambda i: (0, 0)),
                  pl.BlockSpec(wuq.shape, lambda i: (0, 0)),
                  pl.BlockSpec(wuk_t.shape, lambda i: (0, 0, 0)),
                  pl.BlockSpec(sel.shape, lambda i: (0, 0)),
                  pl.BlockSpec((tm, cos_q.shape[1]), lambda i: (i, 0)),
                  pl.BlockSpec((tm, sin_q.shape[1]), lambda i: (i, 0))],
        out_specs=pl.BlockSpec((tm, MLA_HEADS * QK_CAT), lambda i: (i, 0)),
        out_shape=jax.ShapeDtypeStruct((t, MLA_HEADS * QK_CAT), BF16),
        compiler_params=_params(("parallel",)))(hmat, g_q.reshape(1, -1), wuq, wuk_t, sel, cos_q, sin_q)


def _mla_kv_kernel(ckv_ref, kr_ref, g_ref, c_ref, s1_ref, s2_ref, ckv_o, kr_o, kcat_o):
    ckv_n = _rms(ckv_ref[...], g_ref[...])
    ckv_o[...] = ckv_n
    kr = kr_ref[...]
    rot = kr * c_ref[...] + pltpu.roll(kr, 96, 1) * s1_ref[...] + pltpu.roll(kr, 32, 1) * s2_ref[...]
    kr_o[...] = rot[:, :ROPE]
    kcat_o[:, :KV_LORA] = ckv_n.astype(BF16)
    kcat_o[:, KV_LORA:] = rot.astype(BF16)


def mla_kv(hmat, ckv_col, kr_col, g_kv, c_tab, s1_tab, s2_tab, tm):
    t = hmat.shape[0]
    row = lambda i: (i, 0)
    return pl.pallas_call(
        _mla_kv_kernel, grid=(t // tm,),
        in_specs=[pl.BlockSpec((tm, KV_LORA), lambda i: (i, ckv_col)),
                  pl.BlockSpec((tm, 128), lambda i: (i, kr_col)),
                  pl.BlockSpec((1, KV_LORA), lambda i: (0, 0)),
                  pl.BlockSpec((tm, 128), row), pl.BlockSpec((tm, 128), row), pl.BlockSpec((tm, 128), row)],
        out_specs=[pl.BlockSpec((tm, KV_LORA), row), pl.BlockSpec((tm, ROPE), row), pl.BlockSpec((tm, QK_CAT), row)],
        out_shape=[jax.ShapeDtypeStruct((t, KV_LORA), F32), jax.ShapeDtypeStruct((t, ROPE), F32),
                   jax.ShapeDtypeStruct((t, QK_CAT), BF16)],
        compiler_params=_params(("parallel",)))(hmat, hmat, g_kv.reshape(1, -1), c_tab, s1_tab, s2_tab)


def _softplus(z):
    return jnp.maximum(z, 0.0) + jnp.log1p(jnp.exp(-jnp.abs(z)))


def _sb_block(q, k, v, w2, mask, acc, carry):
    s = k.shape[0]
    z = lax.dot_general(q, k, NT, preferred_element_type=F32)
    sp = _softplus(z)
    lf = -sp
    if mask is not None:
        lf = jnp.where(mask, lf, 0.0)
    hi = lf.astype(BF16)
    lo = (lf - hi.astype(F32)).astype(BF16)
    r = q.shape[0]
    sums = jnp.dot(jnp.concatenate([hi, lo], axis=0), w2, preferred_element_type=F32)
    sums = sums[:r] + sums[r:]
    w = jnp.exp(z - sp + sums[:, :s] + carry)
    if mask is not None:
        w = jnp.where(mask, w, 0.0)
    acc = acc + jnp.dot(w.astype(BF16), v, preferred_element_type=F32)
    return acc, carry + sums[:, s:]


def _sb_prompt_kernel(q_ref, k_ref, v_ref, w2_ref, o_ref, *, tq, tk):
    i = pl.program_id(2)
    q = jnp.concatenate([q_ref[:, g * SB_HEAD_DIM:(g + 1) * SB_HEAD_DIM] for g in range(SB_GROUP)], axis=0)
    q = (q * SB_SCALE).astype(BF16)
    r = SB_GROUP * tq
    qpos = i * tq + lax.broadcasted_iota(I32, (r, tk), 0) % tq
    lane = lax.broadcasted_iota(I32, (r, tk), 1)
    w2 = w2_ref[...]
    jmax = (i * tq + tq - 1) // tk

    def body(it, st):
        acc, carry = st
        j = jmax - it
        start = pl.multiple_of(j * tk, tk)
        k = k_ref[pl.ds(start, tk), :].astype(BF16)
        v = v_ref[pl.ds(start, tk), :].astype(BF16)
        mask = (j * tk + lane) < qpos
        return _sb_block(q, k, v, w2, mask, acc, carry)

    acc, _ = lax.fori_loop(0, jmax + 1, body, (jnp.zeros((r, SB_HEAD_DIM), F32), jnp.zeros((r, tk), F32)))
    for g in range(SB_GROUP):
        o_ref[:, g * SB_HEAD_DIM:(g + 1) * SB_HEAD_DIM] = acc[g * tq:(g + 1) * tq].astype(o_ref.dtype)


def _tri_sum_matrix(s):
    j = jnp.arange(s)
    upper = (j[:, None] > j[None, :]).astype(BF16)
    return jnp.concatenate([upper, jnp.ones((s, s), BF16)], axis=1)


def sb_prompt(hmat, batch, seq, tq, tk):
    nq = seq // tq
    gw = SB_GROUP * SB_HEAD_DIM
    k_col0 = SB_HEADS * SB_HEAD_DIM // SB_HEAD_DIM
    v_col0 = k_col0 + SB_KV_HEADS
    return pl.pallas_call(
        functools.partial(_sb_prompt_kernel, tq=tq, tk=tk), grid=(batch, SB_KV_HEADS, nq),
        in_specs=[pl.BlockSpec((tq, gw), lambda b, kv, i: (b * nq + i, kv)),
                  pl.BlockSpec((seq, SB_HEAD_DIM), lambda b, kv, i: (b, k_col0 + kv)),
                  pl.BlockSpec((seq, SB_HEAD_DIM), lambda b, kv, i: (b, v_col0 + kv)),
                  pl.BlockSpec((tk, 2 * tk), lambda b, kv, i: (0, 0))],
        out_specs=pl.BlockSpec((tq, gw), lambda b, kv, i: (b * nq + i, kv)),
        out_shape=jax.ShapeDtypeStruct((batch * seq, SB_HEADS * SB_HEAD_DIM), BF16),
        compiler_params=_params(("parallel", "parallel", "arbitrary")))(hmat, hmat, hmat, _tri_sum_matrix(tk))


def _paged_pipeline(pt_ref, n_pages, ppc, copies_for, compute_chunk, state):
    b = pl.program_id(0)
    nb = pl.num_programs(0)
    nc = n_pages // ppc

    def start_chunk(bb, c, slot):
        for i in range(ppc):
            for cp in copies_for(pt_ref[bb * n_pages + c * ppc + i], slot, i):
                cp.start()

    def wait_chunk(slot):
        for i in range(ppc):
            for cp in copies_for(0, slot, i):
                cp.wait()

    @pl.when(b == 0)
    def _():
        start_chunk(0, 0, 0)

    def body(c, st):
        slot = (b * nc + c) & 1
        more = c + 1 < nc

        @pl.when(more | (b + 1 < nb))
        def _():
            start_chunk(jnp.where(more, b, b + 1), jnp.where(more, c + 1, 0), 1 - slot)

        wait_chunk(slot)
        return compute_chunk(slot, st)

    return lax.fori_loop(0, nc, body, state)


def _sb_sample_kernel(pt_ref, q_ref, kn_ref, vn_ref, w2_ref, ck_hbm, cv_hbm, o_ref, kbuf, vbuf, sem, *,
                      n_pages, ppc, n_new):
    dh = SB_HEAD_DIM
    rows = n_new * SB_GROUP
    q = (q_ref[0] * SB_SCALE).astype(BF16)
    w2 = w2_ref[...]

    def page(kp, vp, mask, st):
        out = []
        for h in range(SB_KV_HEADS):
            k = kp[:, h * dh:(h + 1) * dh].astype(BF16)
            v = vp[:, h * dh:(h + 1) * dh].astype(BF16)
            out.append(_sb_block(q[h], k, v, w2, mask, st[2 * h], st[2 * h + 1]))
        return tuple(x for pair in out for x in pair)

    tpos = lax.broadcasted_iota(I32, (rows, PAGE_SIZE), 0) // SB_GROUP
    mask = lax.broadcasted_iota(I32, (rows, PAGE_SIZE), 1) < tpos
    zero = (jnp.zeros((rows, dh), F32), jnp.zeros((rows, PAGE_SIZE), F32))
    st = page(kn_ref[0], vn_ref[0], mask, zero * SB_KV_HEADS)

    def copies_for(page_id, slot, i):
        return (pltpu.make_async_copy(ck_hbm.at[page_id], kbuf.at[slot, i], sem.at[0, slot]),
                pltpu.make_async_copy(cv_hbm.at[page_id], vbuf.at[slot, i], sem.at[1, slot]))

    def compute_chunk(slot, st):
        for i in range(ppc):
            st = page(kbuf[slot, i], vbuf[slot, i], None, st)
        return st

    st = _paged_pipeline(pt_ref, n_pages, ppc, copies_for, compute_chunk, st)
    for h in range(SB_KV_HEADS):
        o_ref[0, h] = st[2 * h]


def sb_sample(pt_rev, q, k_new, v_new, cache_k, cache_v, ppc):
    db, kvh, rows, dh = q.shape
    n_pages = pt_rev.shape[0] // db
    width = kvh * dh
    grid_spec = pltpu.PrefetchScalarGridSpec(
        num_scalar_prefetch=1, grid=(db,),
        in_specs=[pl.BlockSpec((1, kvh, rows, dh), lambda b, pt: (b, 0, 0, 0)),
                  pl.BlockSpec((1, PAGE_SIZE, width), lambda b, pt: (b, 0, 0)),
                  pl.BlockSpec((1, PAGE_SIZE, width), lambda b, pt: (b, 0, 0)),
                  pl.BlockSpec((PAGE_SIZE, 2 * PAGE_SIZE), lambda b, pt: (0, 0)),
                  pl.BlockSpec(memory_space=pl.ANY), pl.BlockSpec(memory_space=pl.ANY)],
        out_specs=pl.BlockSpec((1, kvh, rows, dh), lambda b, pt: (b, 0, 0, 0)),
        scratch_shapes=[pltpu.VMEM((2, ppc, PAGE_SIZE, width), F32), pltpu.VMEM((2, ppc, PAGE_SIZE, width), F32),
                        pltpu.SemaphoreType.DMA((2, 2))])
    return pl.pallas_call(
        functools.partial(_sb_sample_kernel, n_pages=n_pages, ppc=ppc, n_new=rows // SB_GROUP),
        grid_spec=grid_spec, out_shape=jax.ShapeDtypeStruct(q.shape, F32),
        compiler_params=_params(("arbitrary",)))(pt_rev, q, k_new, v_new, _tri_sum_matrix(PAGE_SIZE), cache_k, cache_v)


def _softmax_block(q, k, mask, m, l, acc):
    s = lax.dot_general(q, k, NT, preferred_element_type=F32)
    if mask is not None:
        s = jnp.where(mask, s, NEG)
    m_new = jnp.maximum(m, jnp.max(s, axis=-1, keepdims=True))
    alpha = jnp.exp(m - m_new)
    p = jnp.exp(s - m_new)
    l = l * alpha + jnp.sum(p, axis=-1, keepdims=True)
    acc = acc * alpha + jnp.dot(p.astype(BF16), k[:, :KV_LORA], preferred_element_type=F32)
    return m_new, l, acc


def _mla_prompt_kernel(q_ref, k_ref, wuv_ref, o_ref, *, tq, tk):
    i = pl.program_id(1)
    qpos = i * tq + lax.broadcasted_iota(I32, (tq, tk), 0)
    lane = lax.broadcasted_iota(I32, (tq, tk), 1)
    nj = (i * tq + tq - 1) // tk + 1
    for h in range(MLA_HEADS):
        q = q_ref[:, h * QK_CAT:(h + 1) * QK_CAT]

        def body(j, st):
            start = pl.multiple_of(j * tk, tk)
            k = k_ref[pl.ds(start, tk), :]
            return _softmax_block(q, k, (j * tk + lane) <= qpos, *st)

        init = (jnp.full((tq, 1), NEG, F32), jnp.zeros((tq, 1), F32), jnp.zeros((tq, KV_LORA), F32))
        _, l, acc = lax.fori_loop(0, nj, body, init)
        lat = (acc / l).astype(BF16)
        o_ref[:, h * V_DIM:(h + 1) * V_DIM] = jnp.dot(lat, wuv_ref[h], preferred_element_type=F32).astype(o_ref.dtype)


def mla_prompt(qcat, kcat, wuv, batch, seq, tq, tk):
    nq = seq // tq
    return pl.pallas_call(
        functools.partial(_mla_prompt_kernel, tq=tq, tk=tk), grid=(batch, nq),
        in_specs=[pl.BlockSpec((tq, MLA_HEADS * QK_CAT), lambda b, i: (b * nq + i, 0)),
                  pl.BlockSpec((seq, QK_CAT), lambda b, i: (b, 0)),
                  pl.BlockSpec(wuv.shape, lambda b, i: (0, 0, 0))],
        out_specs=pl.BlockSpec((tq, MLA_HEADS * V_DIM), lambda b, i: (b * nq + i, 0)),
        out_shape=jax.ShapeDtypeStruct((batch * seq, MLA_HEADS * V_DIM), BF16),
        compiler_params=_params(("parallel", "arbitrary")))(qcat, kcat, wuv)


def _mla_sample_kernel(pt_ref, q_ref, kn_ref, c_hbm, r_hbm, o_ref, cbuf, rbuf, sem, *, n_pages, ppc, n_new):
    rows = n_new * MLA_HEADS
    q = q_ref[0]
    tpos = lax.broadcasted_iota(I32, (rows, PAGE_SIZE), 0) // MLA_HEADS
    mask = lax.broadcasted_iota(I32, (rows, PAGE_SIZE), 1) <= tpos
    init = (jnp.full((rows, 1), NEG, F32), jnp.zeros((rows, 1), F32), jnp.zeros((rows, KV_LORA), F32))
    st = _softmax_block(q, kn_ref[0], mask, *init)

    def copies_for(page_id, slot, i):
        return (pltpu.make_async_copy(c_hbm.at[page_id], cbuf.at[slot, i], sem.at[0, slot]),
                pltpu.make_async_copy(r_hbm.at[page_id], rbuf.at[slot, i], sem.at[1, slot]))

    pad = jnp.zeros((PAGE_SIZE, QK_CAT - KV_LORA - ROPE), BF16)

    def compute_chunk(slot, st):
        for i in range(ppc):
            k = jnp.concatenate([cbuf[slot, i].astype(BF16), rbuf[slot, i].astype(BF16), pad], axis=-1)
            st = _softmax_block(q, k, None, *st)
        return st

    _, l, acc = _paged_pipeline(pt_ref, n_pages, ppc, copies_for, compute_chunk, st)
    o_ref[0] = (acc / l).astype(o_ref.dtype)


def mla_sample(pt, q, k_new, cache_ckv, cache_kr, ppc):
    db, rows, _ = q.shape
    n_pages = pt.shape[0] // db
    grid_spec = pltpu.PrefetchScalarGridSpec(
        num_scalar_prefetch=1, grid=(db,),
        in_specs=[pl.BlockSpec((1, rows, QK_CAT), lambda b, pt: (b, 0, 0)),
                  pl.BlockSpec((1, PAGE_SIZE, QK_CAT), lambda b, pt: (b, 0, 0)),
                  pl.BlockSpec(memory_space=pl.ANY), pl.BlockSpec(memory_space=pl.ANY)],
        out_specs=pl.BlockSpec((1, rows, KV_LORA), lambda b, pt: (b, 0, 0)),
        scratch_shapes=[pltpu.VMEM((2, ppc, PAGE_SIZE, KV_LORA), F32), pltpu.VMEM((2, ppc, PAGE_SIZE, ROPE), F32),
                        pltpu.SemaphoreType.DMA((2, 2))])
    return pl.pallas_call(
        functools.partial(_mla_sample_kernel, n_pages=n_pages, ppc=ppc, n_new=rows // MLA_HEADS),
        grid_spec=grid_spec, out_shape=jax.ShapeDtypeStruct((db, rows, KV_LORA), BF16),
        compiler_params=_params(("arbitrary",)))(pt, q, k_new, cache_ckv, cache_kr)


def _head_mm_kernel(a_ref, w_ref, o_ref):
    o_ref[...] = jnp.dot(a_ref[...], w_ref[0], preferred_element_type=F32).astype(o_ref.dtype)


def head_matmul(a, w, out_dtype):
    m = a.shape[0]
    h, k, n = w.shape
    return pl.pallas_call(
        _head_mm_kernel, grid=(h,),
        in_specs=[pl.BlockSpec((m, k), lambda i: (0, i)), pl.BlockSpec((1, k, n), lambda i: (i, 0, 0))],
        out_specs=pl.BlockSpec((m, n), lambda i: (0, i)),
        out_shape=jax.ShapeDtypeStruct((m, h * n), out_dtype),
        compiler_params=_params(("parallel",)))(a, w)


MEM_PAIR = 2


def _mem_heads(q, k, v):
    outs = []
    for h in range(MEM_PAIR):
        sl = slice(h * MEM_DIM, (h + 1) * MEM_DIM)
        qh = (q[:, sl] * MEM_SCALE).astype(BF16)
        s = lax.dot_general(qh, k[:, sl].astype(BF16), NT, preferred_element_type=F32)
        p = jnp.exp(s - jnp.max(s, axis=-1, keepdims=True))
        o = jnp.dot(p.astype(BF16), v[:, sl].astype(BF16), preferred_element_type=F32)
        outs.append(o / jnp.sum(p, axis=-1, keepdims=True))
    return jnp.concatenate(outs, axis=-1)


def _mem_prompt_kernel(q_ref, k_ref, v_ref, o_ref):
    o_ref[...] = _mem_heads(q_ref[...], k_ref[...], v_ref[...]).astype(o_ref.dtype)


def mem_attn_prompt(hmat, memkv, q_col0, batch, seq, n_mem, tq):
    nq = seq // tq
    w = MEM_PAIR * MEM_DIM
    n_pair = MEM_HEADS // MEM_PAIR
    return pl.pallas_call(
        _mem_prompt_kernel, grid=(batch, n_pair, nq),
        in_specs=[pl.BlockSpec((tq, w), lambda b, hp, i: (b * nq + i, q_col0 + hp)),
                  pl.BlockSpec((n_mem, w), lambda b, hp, i: (b, hp)),
                  pl.BlockSpec((n_mem, w), lambda b, hp, i: (b, n_pair + hp))],
        out_specs=pl.BlockSpec((tq, w), lambda b, hp, i: (b * nq + i, hp)),
        out_shape=jax.ShapeDtypeStruct((batch * seq, MEM_HEADS * MEM_DIM), BF16),
        compiler_params=_params(("parallel", "parallel", "arbitrary")))(hmat, memkv, memkv)


def _mem_sample_kernel(q_ref, k_ref, v_ref, o_ref, *, n_new, group):
    q = q_ref[...]
    row = lax.broadcasted_iota(I32, q.shape, 0) // n_new
    out = jnp.zeros(q.shape, F32)
    for g in range(group):
        out = jnp.where(row == g, _mem_heads(q, k_ref[g], v_ref[g]), out)
    o_ref[...] = out.astype(o_ref.dtype)


def mem_sample(hmat, cache_k, cache_v, q_col0, row0, n_new):
    db, n_mem, _ = cache_k.shape
    group = 8 // n_new
    rows = group * n_new
    w = MEM_PAIR * MEM_DIM
    n_pair = MEM_HEADS // MEM_PAIR
    r0 = row0 // rows
    return pl.pallas_call(
        functools.partial(_mem_sample_kernel, n_new=n_new, group=group), grid=(n_pair, db // group),
        in_specs=[pl.BlockSpec((rows, w), lambda hp, i: (r0 + i, q_col0 + hp)),
                  pl.BlockSpec((group, n_mem, w), lambda hp, i: (i, 0, hp)),
                  pl.BlockSpec((group, n_mem, w), lambda hp, i: (i, 0, hp))],
        out_specs=pl.BlockSpec((rows, w), lambda hp, i: (i, hp)),
        out_shape=jax.ShapeDtypeStruct((db * n_new, MEM_HEADS * MEM_DIM), BF16),
        compiler_params=_params(("parallel", "arbitrary")))(hmat, cache_k, cache_v)


def _mix_kernel(osb_ref, omla_ref, omem_ref, wsb_ref, wmla_ref, wmem_ref, gsb_ref, gmla_ref, gmem_ref, o_ref):
    mix = jnp.zeros(o_ref.shape, F32)
    for o, w, g in ((osb_ref, wsb_ref, gsb_ref), (omla_ref, wmla_ref, gmla_ref), (omem_ref, wmem_ref, gmem_ref)):
        mix = mix + jax.nn.sigmoid(g[...]) * jnp.dot(o[...], w[...], preferred_element_type=F32)
    o_ref[...] = mix.astype(o_ref.dtype)


def branch_mix(o_sb, o_mla, o_mem, w_sb, w_mla, w_mem, hmat, gate_col0, d, tm, tn):
    t, kdim = o_sb.shape
    nn = d // tn
    a_spec = pl.BlockSpec((tm, kdim), lambda i, j: (i, 0))
    w_spec = pl.BlockSpec((kdim, tn), lambda i, j: (0, j))
    g_specs = [pl.BlockSpec((tm, tn), lambda i, j, br=br: (i, gate_col0 + br * nn + j)) for br in range(3)]
    return pl.pallas_call(
        _mix_kernel, grid=(t // tm, nn),
        in_specs=[a_spec, a_spec, a_spec, w_spec, w_spec, w_spec] + g_specs,
        out_specs=pl.BlockSpec((tm, tn), lambda i, j: (i, j)),
        out_shape=jax.ShapeDtypeStruct((t, d), BF16),
        compiler_params=_params(("parallel", "arbitrary")))(o_sb, o_mla, o_mem, w_sb, w_mla, w_mem, hmat, hmat, hmat)


def _out_proj_kernel(mix_ref, w_ref, x_ref, g_ref, rhi_ref, rlo_ref, b_ref, xmid_ref, hn_ref, lg_ref):
    xmid = x_ref[...] + jnp.dot(mix_ref[...], w_ref[...], preferred_element_type=F32)
    xmid_ref[...] = xmid
    hn = _rms(xmid, g_ref[...])
    hn_ref[...] = hn
    hi = hn.astype(BF16)
    lo = (hn - hi.astype(F32)).astype(BF16)
    lg = (jnp.dot(hi, rhi_ref[...], preferred_element_type=F32) + jnp.dot(lo, rhi_ref[...], preferred_element_type=F32)
          + jnp.dot(hi, rlo_ref[...], preferred_element_type=F32))
    lg_ref[...] = lg + b_ref[...]


def out_proj(mix, w_out, x, g_ffn, r_hi, r_lo, r_bias, tm):
    t, d = x.shape
    row = lambda i: (i, 0)
    fix = lambda i: (0, 0)
    return pl.pallas_call(
        _out_proj_kernel, grid=(t // tm,),
        in_specs=[pl.BlockSpec((tm, d), row), pl.BlockSpec((d, d), fix), pl.BlockSpec((tm, d), row),
                  pl.BlockSpec((1, d), fix), pl.BlockSpec((d, 128), fix), pl.BlockSpec((d, 128), fix),
                  pl.BlockSpec((1, 128), fix)],
        out_specs=[pl.BlockSpec((tm, d), row), pl.BlockSpec((tm, d), row), pl.BlockSpec((tm, 128), row)],
        out_shape=[jax.ShapeDtypeStruct((t, d), F32), jax.ShapeDtypeStruct((t, d), F32),
                   jax.ShapeDtypeStruct((t, 128), F32)],
        compiler_params=_params(("parallel",)))(mix, w_out, x, g_ffn.reshape(1, d), r_hi, r_lo, r_bias)


def _route_kernel(lg_ref, o_ref):
    lg = lg_ref[...]
    lane = lax.broadcasted_iota(I32, lg.shape, 1)
    lanef = lane.astype(F32)

    def first_max(x):
        v = jnp.max(x, axis=-1, keepdims=True)
        return v, jnp.min(jnp.where(x == v, lanef, 1e6), axis=-1, keepdims=True)

    is_grp = lane < N_GROUPS
    gl = jnp.where(is_grp, lg, NEG)
    gmax, g_idx = first_max(gl)
    g_p = 1.0 / jnp.sum(jnp.where(is_grp, jnp.exp(gl - gmax), 0.0), axis=-1, keepdims=True)
    grp_of_lane = ((lane - N_GROUPS) >> 3).astype(F32)
    el = jnp.where((lane >= N_GROUPS) & (grp_of_lane == g_idx), lg, NEG)
    v1, i1 = first_max(el)
    v2, i2 = first_max(jnp.where(lanef == i1, NEG, el))
    d = jnp.exp(v2 - v1)
    w1 = g_p / (1.0 + d)
    w2 = g_p * d / (1.0 + d)
    e1 = i1 - N_GROUPS
    e2 = i2 - N_GROUPS
    o_ref[...] = jnp.where(lane == 0, e1, jnp.where(lane == 1, e2, jnp.where(lane == 2, w1, jnp.where(lane == 3, w2, 0.0))))


def route(logits, tm):
    t = logits.shape[0]
    return pl.pallas_call(
        _route_kernel, grid=(t // tm,),
        in_specs=[pl.BlockSpec((tm, 128), lambda i: (i, 0))], out_specs=pl.BlockSpec((tm, 128), lambda i: (i, 0)),
        out_shape=jax.ShapeDtypeStruct((t, 128), F32), compiler_params=_params(("parallel",)))(logits)


def _row_copies(idx_ref, base, n, src_hbm, dst, sem):
    return [pltpu.make_async_copy(src_hbm.at[pl.ds(idx_ref[base + r], 1)], dst.at[pl.ds(r, 1)], sem)
            for r in range(n)]


def _gather_kernel(tok_ref, nused_ref, h_hbm, o_ref, sem):
    u = pl.program_id(0)

    @pl.when(u < nused_ref[0])
    def _():
        cps = _row_copies(tok_ref, u * MOE_BLOCK, MOE_BLOCK, h_hbm, o_ref, sem.at[0])
        for cp in cps:
            cp.start()
        for cp in cps:
            cp.wait()

    @pl.when(u >= nused_ref[0])
    def _():
        o_ref[...] = jnp.zeros(o_ref.shape, o_ref.dtype)


def moe_gather(slot_tok, n_used, hn):
    n_slots = slot_tok.shape[0]
    d = hn.shape[1]
    blk = lambda u, tok, nu: (u, 0)
    grid_spec = pltpu.PrefetchScalarGridSpec(
        num_scalar_prefetch=2, grid=(n_slots // MOE_BLOCK,),
        in_specs=[pl.BlockSpec(memory_space=pl.ANY)],
        out_specs=pl.BlockSpec((MOE_BLOCK, d), blk),
        scratch_shapes=[pltpu.SemaphoreType.DMA((1,))])
    return pl.pallas_call(_gather_kernel, grid_spec=grid_spec, out_shape=jax.ShapeDtypeStruct((n_slots, d), hn.dtype),
                          compiler_params=_params(("arbitrary",)))(slot_tok, n_used, hn)


def _expert_kernel(be_ref, nused_ref, x_ref, wg_ref, wu_ref, wd_ref, o_ref):
    @pl.when(pl.program_id(0) < nused_ref[0])
    def _():
        x = x_ref[...].astype(BF16)
        gate = jnp.dot(x, wg_ref[0].astype(BF16), preferred_element_type=F32)
        up = jnp.dot(x, wu_ref[0].astype(BF16), preferred_element_type=F32)
        a = (gate * jax.nn.sigmoid(gate) * up).astype(BF16)
        o_ref[...] = jnp.dot(a, wd_ref[0].astype(BF16), preferred_element_type=F32)

    @pl.when(pl.program_id(0) >= nused_ref[0])
    def _():
        o_ref[...] = jnp.zeros(o_ref.shape, o_ref.dtype)


def moe_experts(block_e, n_used, xs, w_gate, w_up, w_down):
    n_slots, d = xs.shape
    de = w_gate.shape[2]
    blk = lambda u, be, nu: (u, 0)
    wsel = lambda u, be, nu: (be[jnp.minimum(u, nu[0] - 1)], 0, 0)
    grid_spec = pltpu.PrefetchScalarGridSpec(
        num_scalar_prefetch=2, grid=(n_slots // MOE_BLOCK,),
        in_specs=[pl.BlockSpec((MOE_BLOCK, d), blk), pl.BlockSpec((1, d, de), wsel), pl.BlockSpec((1, d, de), wsel),
                  pl.BlockSpec((1, de, d), wsel)],
        out_specs=pl.BlockSpec((MOE_BLOCK, d), blk))
    return pl.pallas_call(_expert_kernel, grid_spec=grid_spec, out_shape=jax.ShapeDtypeStruct((n_slots, d), F32),
                          compiler_params=_params(("arbitrary",)))(block_e, n_used, xs, w_gate, w_up, w_down)


def _combine_kernel(s1_ref, s2_ref, x_ref, rt_ref, g_ref, y_hbm, o_ref, ybuf, sem, *, tm):
    base = pl.program_id(0) * tm
    cps = (_row_copies(s1_ref, base, tm, y_hbm, ybuf.at[0], sem.at[0])
           + _row_copies(s2_ref, base, tm, y_hbm, ybuf.at[1], sem.at[0]))
    for cp in cps:
        cp.start()
    for cp in cps:
        cp.wait()
    rt = rt_ref[...]
    x = x_ref[...] + (rt[:, 2:3] * ybuf[0] + rt[:, 3:4] * ybuf[1])
    o_ref[...] = _rms(x, g_ref[...])


def moe_combine(slot1, slot2, xmid, routed, g_final, ys, tm):
    t, d = xmid.shape
    grid_spec = pltpu.PrefetchScalarGridSpec(
        num_scalar_prefetch=2, grid=(t // tm,),
        in_specs=[pl.BlockSpec((tm, d), lambda i, a, b: (i, 0)), pl.BlockSpec((tm, 128), lambda i, a, b: (i, 0)),
                  pl.BlockSpec((1, d), lambda i, a, b: (0, 0)), pl.BlockSpec(memory_space=pl.ANY)],
        out_specs=pl.BlockSpec((tm, d), lambda i, a, b: (i, 0)),
        scratch_shapes=[pltpu.VMEM((2, tm, d), F32), pltpu.SemaphoreType.DMA((1,))])
    return pl.pallas_call(functools.partial(_combine_kernel, tm=tm), grid_spec=grid_spec,
                          out_shape=jax.ShapeDtypeStruct((t, d), F32),
                          compiler_params=_params(("arbitrary",)))(slot1, slot2, xmid, routed, g_final.reshape(1, d), ys)


def _dispatch_plan(e12, n_exp):
    t = e12.shape[0]
    flat_e = e12.reshape(-1)
    tk = flat_e.shape[0]
    n_blocks = -(-tk // MOE_BLOCK) + n_exp
    order = jnp.argsort(flat_e).astype(I32)
    se = flat_e[order]
    counts = jnp.bincount(flat_e, length=n_exp).astype(I32)
    start = jnp.cumsum(counts) - counts
    padded = (counts + MOE_BLOCK - 1) // MOE_BLOCK * MOE_BLOCK
    pad_end = jnp.cumsum(padded)
    dest = (pad_end - padded)[se] + jnp.arange(tk, dtype=I32) - start[se]
    slot_of = jnp.zeros((tk,), I32).at[order].set(dest).reshape(t, 2)
    slot_tok = jnp.zeros((n_blocks * MOE_BLOCK,), I32).at[dest].set(order // 2)
    block_e = jnp.minimum(jnp.searchsorted(pad_end, jnp.arange(n_blocks, dtype=I32) * MOE_BLOCK, side='right'),
                          n_exp - 1).astype(I32)
    n_used = (pad_end[-1:] // MOE_BLOCK).astype(I32)
    return slot_of, slot_tok, block_e, n_used


def _tile(n, target, mult):
    best = None
    for c in range(mult, min(n, target) + 1, mult):
        if n % c == 0:
            best = c
    assert best is not None, (n, target, mult)
    return best


def kernel(x_prompt, x_sample, mem_prompt, cache_sb_k, cache_sb_v, cache_mla_ckv, cache_mla_krope, cache_mem_k, cache_mem_v, page_table, g_attn_norm, w_in, g_q_norm, w_uq, g_kv_norm, w_uk, w_uv, g_mem_norm, w_mem_kv, w_br_sb, w_br_mla, w_br_mem, w_out, g_ffn_norm, w_grp, b_grp, w_rtr, b_rtr, w_gate, w_up, w_down, g_final):
    depth = g_attn_norm.shape[0]
    assert depth == 1
    batch, seq, d = x_prompt.shape
    db, n_new, _ = x_sample.shape
    n_mem = mem_prompt.shape[1]
    n_pages = page_table.shape[1]
    n_pool = cache_sb_k.shape[1]
    tp, ts = batch * seq, db * n_new
    t = tp + ts
    l = 0

    sbq_w, sbkv_w = SB_HEADS * SB_HEAD_DIM, SB_KV_HEADS * SB_HEAD_DIM
    memq_w = MEM_HEADS * MEM_DIM
    c_kr = 2 * sbkv_w + sbq_w + Q_LORA + KV_LORA
    w = w_in[l]
    wp = jnp.concatenate([w[:, :c_kr], w[:, c_kr + ROPE:], w[:, c_kr:c_kr + ROPE], jnp.zeros((d, 128 - ROPE), F32)],
                         axis=1).astype(BF16)
    col_sbq, col_cq = 0, sbq_w + 2 * sbkv_w
    col_ckv = col_cq + Q_LORA
    col_memq = col_ckv + KV_LORA
    col_gate = col_memq + memq_w
    col_kr = col_gate + 3 * d
    n_cols = wp.shape[1]

    half = ROPE // 2
    inv = ROPE_THETA ** (-jnp.arange(half, dtype=F32) / half)
    pos = jnp.concatenate([jnp.tile(jnp.arange(seq), batch), jnp.tile(n_pages * PAGE_SIZE + jnp.arange(n_new), db)])
    ang = pos.astype(F32)[:, None] * inv[None, :]
    cos, sin = jnp.cos(ang), jnp.sin(ang)
    zeros = jnp.zeros_like(cos)
    cos_q, sin_q = jnp.tile(cos, (1, MLA_HEADS)), jnp.tile(sin, (1, MLA_HEADS))
    c_tab = jnp.concatenate([cos, cos, zeros, zeros], axis=1)
    s1_tab = jnp.concatenate([-sin, zeros, zeros, zeros], axis=1)
    s2_tab = jnp.concatenate([zeros, sin, zeros, zeros], axis=1)

    wuq = w_uq[l]
    wuq2 = jnp.concatenate([wuq[:, :, :NOPE].reshape(Q_LORA, -1), wuq[:, :, NOPE:NOPE + half].reshape(Q_LORA, -1),
                            wuq[:, :, NOPE + half:].reshape(Q_LORA, -1)], axis=1).astype(BF16)
    wuk_t = jnp.transpose(w_uk[l], (1, 2, 0)).astype(BF16)
    wuv = jnp.transpose(w_uv[l], (1, 0, 2)).astype(BF16)
    hh = jnp.arange(MLA_HEADS)[:, None]
    ii = jnp.arange(half)[None, :]
    sel = jnp.zeros((2 * MLA_HEADS * half, MLA_HEADS * 128), BF16)
    sel = sel.at[(hh * half + ii).ravel(), (hh * 128 + ii).ravel()].set(1.0)
    sel = sel.at[(MLA_HEADS * half + hh * half + ii).ravel(), (hh * 128 + half + ii).ravel()].set(1.0)

    xcat = jnp.concatenate([x_prompt.reshape(tp, d), x_sample.reshape(ts, d)], axis=0)
    tm_big = _tile(t, 1152, 16)
    hn_attn = rmsnorm(xcat, g_attn_norm[l], BF16, _tile(t, 640, 16))
    hmat = matmul(hn_attn, wp, F32, tm_big, _tile(n_cols, 1024, 128))

    tm_mid = _tile(t, 640, 16)
    qcat = mla_q(hmat, col_cq // Q_LORA, g_q_norm[l], wuq2, wuk_t, sel, cos_q, sin_q, tm_mid)
    ckv_n, kr_rot, kcat = mla_kv(hmat, col_ckv // KV_LORA, col_kr // 128, g_kv_norm[l], c_tab, s1_tab, s2_tab, tm_mid)

    memn = rmsnorm(mem_prompt.reshape(batch * n_mem, d), g_mem_norm[l], BF16, _tile(batch * n_mem, 512, 16))
    memkv = matmul(memn, w_mem_kv[l].astype(BF16), F32, _tile(batch * n_mem, 1024, 16), 1024)

    o_sb_p = sb_prompt(hmat, batch, seq, 128, 256)
    o_mla_p = mla_prompt(qcat, kcat, wuv, batch, seq, 256, 256)
    o_mem_p = mem_attn_prompt(hmat, memkv, col_memq // (MEM_PAIR * MEM_DIM), batch, seq, n_mem, _tile(seq, 512, 8))

    hs = hmat[tp:]
    rows_sb = n_new * SB_GROUP
    q_sb = hs[:, :sbq_w].reshape(db, n_new, SB_KV_HEADS, SB_GROUP, SB_HEAD_DIM)
    q_sb = jnp.transpose(q_sb, (0, 2, 1, 3, 4)).reshape(db, SB_KV_HEADS, rows_sb, SB_HEAD_DIM)
    pad_new = ((0, 0), (0, PAGE_SIZE - n_new), (0, 0))
    k_new = jnp.pad(hs[:, sbq_w:sbq_w + sbkv_w].reshape(db, n_new, sbkv_w), pad_new)
    v_new = jnp.pad(hs[:, sbq_w + sbkv_w:sbq_w + 2 * sbkv_w].reshape(db, n_new, sbkv_w), pad_new)
    pt = page_table.astype(I32)
    ppc = _tile(n_pages, 8, 2) if n_pages % 2 == 0 else 1
    o_sb_s = sb_sample(pt[:, ::-1].reshape(-1), q_sb, k_new, v_new,
                       cache_sb_k[l].reshape(n_pool, PAGE_SIZE, sbkv_w), cache_sb_v[l].reshape(n_pool, PAGE_SIZE, sbkv_w), ppc)
    o_sb_s = jnp.transpose(o_sb_s.reshape(db, SB_KV_HEADS, n_new, SB_GROUP, SB_HEAD_DIM), (0, 2, 1, 3, 4))
    o_sb_s = o_sb_s.reshape(ts, sbq_w).astype(BF16)

    q_mla_s = qcat[tp:].reshape(db, n_new * MLA_HEADS, QK_CAT)
    kcat_new = jnp.pad(kcat[tp:].reshape(db, n_new, QK_CAT), pad_new)
    lat_s = mla_sample(pt.reshape(-1), q_mla_s, kcat_new, cache_mla_ckv[l], cache_mla_krope[l], ppc)
    o_mla_s = head_matmul(lat_s.reshape(ts, MLA_HEADS * KV_LORA), wuv, BF16)

    o_mem_s = mem_sample(hmat, cache_mem_k[l].reshape(db, n_mem, memq_w), cache_mem_v[l].reshape(db, n_mem, memq_w),
                         col_memq // (MEM_PAIR * MEM_DIM), tp, n_new)

    o_sb = jnp.concatenate([o_sb_p, o_sb_s], axis=0)
    o_mla = jnp.concatenate([o_mla_p, o_mla_s], axis=0)
    o_mem = jnp.concatenate([o_mem_p, o_mem_s], axis=0)
    tn_mix = 512
    mix = branch_mix(o_sb, o_mla, o_mem, w_br_sb[l].astype(BF16), w_br_mla[l].astype(BF16), w_br_mem[l].astype(BF16),
                     hmat, col_gate // tn_mix, d, tm_big, tn_mix)
    w_r = jnp.concatenate([w_grp[l], w_rtr[l], jnp.zeros((d, 128 - N_GROUPS - N_EXPERTS), F32)], axis=1)
    r_hi = w_r.astype(BF16)
    r_lo = (w_r - r_hi.astype(F32)).astype(BF16)
    r_bias = jnp.concatenate([b_grp[l], b_rtr[l], jnp.zeros((128 - N_GROUPS - N_EXPERTS,), F32)]).reshape(1, 128)
    xmid, hn_ffn, logits = out_proj(mix, w_out[l].astype(BF16), xcat, g_ffn_norm[l], r_hi, r_lo, r_bias, _tile(t, 320, 16))

    routed = route(logits, _tile(t, 1152, 8))
    slot_of, slot_tok, block_e, n_used = _dispatch_plan(routed[:, :2].astype(I32), N_EXPERTS)
    xs = moe_gather(slot_tok, n_used, hn_ffn)
    ys = moe_experts(block_e, n_used, xs, w_gate[l], w_up[l], w_down[l])
    y = moe_combine(slot_of[:, 0], slot_of[:, 1], xmid, routed, g_final, ys, _tile(t, 128, 8))

    def kv_out(cols, rows, shape):
        return rows[:, cols[0]:cols[1]].reshape(shape)[None]

    hp = hmat[:tp]
    return (y[:tp].reshape(batch, seq, d), y[tp:].reshape(db, n_new, d),
            kv_out((sbq_w, sbq_w + sbkv_w), hp, (batch, seq, SB_KV_HEADS, SB_HEAD_DIM)),
            kv_out((sbq_w + sbkv_w, sbq_w + 2 * sbkv_w), hp, (batch, seq, SB_KV_HEADS, SB_HEAD_DIM)),
            ckv_n[:tp].reshape(batch, seq, KV_LORA)[None], kr_rot[:tp].reshape(batch, seq, ROPE)[None],
            memkv[:, :memq_w].reshape(batch, n_mem, MEM_HEADS, MEM_DIM)[None],
            memkv[:, memq_w:].reshape(batch, n_mem, MEM_HEADS, MEM_DIM)[None],
            kv_out((sbq_w, sbq_w + sbkv_w), hs, (db, n_new, SB_KV_HEADS, SB_HEAD_DIM)),
            kv_out((sbq_w + sbkv_w, sbq_w + 2 * sbkv_w), hs, (db, n_new, SB_KV_HEADS, SB_HEAD_DIM)),
            ckv_n[tp:].reshape(db, n_new, KV_LORA)[None], kr_rot[tp:].reshape(db, n_new, ROPE)[None])
```

```python
import functools

import jax
import jax.numpy as jnp
from jax import lax
from jax.experimental import pallas as pl
from jax.experimental.pallas import tpu as pltpu

F32 = jnp.float32
BF16 = jnp.bfloat16
I32 = jnp.int32

NORM_EPS = 1e-6
ROPE_THETA = 10000.0
PAGE_SIZE = 128
LANES = 128
SB_HEADS, SB_KV_HEADS, SB_HEAD_DIM = 8, 2, 128
SB_GROUP = SB_HEADS // SB_KV_HEADS
SB_SCALE = SB_HEAD_DIM ** -0.5
MLA_HEADS, Q_LORA, KV_LORA, NOPE, ROPE, V_DIM = 8, 512, 512, 128, 64, 128
MLA_SCALE = (NOPE + ROPE) ** -0.5
QK_CAT = KV_LORA + LANES
MEM_HEADS, MEM_DIM = 4, 256
MEM_SCALE = MEM_DIM ** -0.5
N_GROUPS, EPG, N_EXPERTS, D_EXPERT = 8, 8, 64, 512
MOE_BLOCK = 128
NEG = -1e30
VMEM_LIMIT = 56 * 1024 * 1024
NT = (((1,), (1,)), ((), ()))


def _params(sem):
    return pltpu.CompilerParams(dimension_semantics=sem, vmem_limit_bytes=VMEM_LIMIT)


def _rms(x, g):
    return x * lax.rsqrt(jnp.mean(x * x, axis=-1, keepdims=True) + NORM_EPS) * g


def _lanes(x, n):
    return x if n == LANES else jnp.concatenate([x] * (n // LANES), axis=1)


def _rmsnorm_kernel(x_ref, g_ref, o_ref):
    o_ref[...] = _rms(x_ref[...], g_ref[...]).astype(o_ref.dtype)


def rmsnorm(x, g, out_dtype, tm):
    r, d = x.shape
    return pl.pallas_call(
        _rmsnorm_kernel, grid=(r // tm,),
        in_specs=[pl.BlockSpec((tm, d), lambda i: (i, 0)), pl.BlockSpec((1, d), lambda i: (0, 0))],
        out_specs=pl.BlockSpec((tm, d), lambda i: (i, 0)),
        out_shape=jax.ShapeDtypeStruct((r, d), out_dtype),
        compiler_params=_params(("parallel",)))(x, g.reshape(1, d))


def _mm_kernel(a_ref, b_ref, o_ref):
    o_ref[...] = jnp.dot(a_ref[...], b_ref[...], preferred_element_type=F32).astype(o_ref.dtype)


def matmul(a, b, out_dtype, tm, tn):
    m, k = a.shape
    n = b.shape[1]
    return pl.pallas_call(
        _mm_kernel, grid=(m // tm, n // tn),
        in_specs=[pl.BlockSpec((tm, k), lambda i, j: (i, 0)), pl.BlockSpec((k, tn), lambda i, j: (0, j))],
        out_specs=pl.BlockSpec((tm, tn), lambda i, j: (i, j)),
        out_shape=jax.ShapeDtypeStruct((m, n), out_dtype),
        compiler_params=_params(("parallel", "arbitrary")))(a, b)


def _mla_q_kernel(cq_ref, g_ref, wuq_ref, wuk_ref, sel_ref, cos_ref, sin_ref, o_ref):
    cqn = _rms(cq_ref[...], g_ref[...]).astype(BF16)
    q = jnp.dot(cqn, wuq_ref[...], preferred_element_type=F32) * MLA_SCALE
    n_np = MLA_HEADS * NOPE
    half = MLA_HEADS * (ROPE // 2)
    r1 = q[:, n_np:n_np + half]
    r2 = q[:, n_np + half:n_np + 2 * half]
    cos, sin = cos_ref[...], sin_ref[...]
    rot = jnp.concatenate([r1 * cos - r2 * sin, r1 * sin + r2 * cos], axis=-1).astype(BF16)
    tail = jnp.dot(rot, sel_ref[...], preferred_element_type=F32)
    for h in range(MLA_HEADS):
        nope = q[:, h * NOPE:(h + 1) * NOPE].astype(BF16)
        lat = jnp.dot(nope, wuk_ref[h], preferred_element_type=F32)
        o_ref[:, h * QK_CAT:h * QK_CAT + KV_LORA] = lat.astype(o_ref.dtype)
        o_ref[:, h * QK_CAT + KV_LORA:(h + 1) * QK_CAT] = tail[:, h * LANES:(h + 1) * LANES].astype(o_ref.dtype)


def mla_q(hmat, cq_col, g_q, wuq, wuk_t, sel, cos_q, sin_q, tm):
    t = hmat.shape[0]
    return pl.pallas_call(
        _mla_q_kernel, grid=(t // tm,),
        in_specs=[pl.BlockSpec((tm, Q_LORA), lambda i: (i, cq_col)),
                  pl.BlockSpec((1, Q_LORA), lambda i: (0, 0)),
                  pl.BlockSpec(wuq.shape, lambda i: (0, 0)),
                  pl.BlockSpec(wuk_t.shape, lambda i: (0, 0, 0)),
                  pl.BlockSpec(sel.shape, lambda i: (0, 0)),
                  pl.BlockSpec((tm, cos_q.shape[1]), lambda i: (i, 0)),
                  pl.BlockSpec((tm, sin_q.shape[1]), lambda i: (i, 0))],
        out_specs=pl.BlockSpec((tm, MLA_HEADS * QK_CAT), lambda i: (i, 0)),
        out_shape=jax.ShapeDtypeStruct((t, MLA_HEADS * QK_CAT), BF16),
        compiler_params=_params(("parallel",)))(hmat, g_q.reshape(1, -1), wuq, wuk_t, sel, cos_q, sin_q)


def _mla_kv_kernel(ckv_ref, kr_ref, g_ref, c_ref, s1_ref, s2_ref, ckv_o, kr_o, kcat_o):
    ckv_n = _rms(ckv_ref[...], g_ref[...])
    ckv_o[...] = ckv_n
    kr = kr_ref[...]
    rot = kr * c_ref[...] + pltpu.roll(kr, 96, 1) * s1_ref[...] + pltpu.roll(kr, 32, 1) * s2_ref[...]
    kr_o[...] = rot[:, :ROPE]
    kcat_o[:, :KV_LORA] = ckv_n.astype(BF16)
    kcat_o[:, KV_LORA:] = rot.astype(BF16)


def mla_kv(hmat, ckv_col, kr_col, g_kv, c_tab, s1_tab, s2_tab, tm):
    t = hmat.shape[0]
    row = lambda i: (i, 0)
    return pl.pallas_call(
        _mla_kv_kernel, grid=(t // tm,),
        in_specs=[pl.BlockSpec((tm, KV_LORA), lambda i: (i, ckv_col)),
                  pl.BlockSpec((tm, LANES), lambda i: (i, kr_col)),
                  pl.BlockSpec((1, KV_LORA), lambda i: (0, 0)),
                  pl.BlockSpec((tm, LANES), row), pl.BlockSpec((tm, LANES), row), pl.BlockSpec((tm, LANES), row)],
        out_specs=[pl.BlockSpec((tm, KV_LORA), row), pl.BlockSpec((tm, ROPE), row), pl.BlockSpec((tm, QK_CAT), row)],
        out_shape=[jax.ShapeDtypeStruct((t, KV_LORA), F32), jax.ShapeDtypeStruct((t, ROPE), F32),
                   jax.ShapeDtypeStruct((t, QK_CAT), BF16)],
        compiler_params=_params(("parallel",)))(hmat, hmat, g_kv.reshape(1, -1), c_tab, s1_tab, s2_tab)


def _softplus(z):
    return jnp.maximum(z, 0.0) + jnp.log1p(jnp.exp(-jnp.abs(z)))


def _tri_sum_matrix(s):
    j = jnp.arange(s)
    upper = (j[:, None] > j[None, :]).astype(BF16)
    return jnp.concatenate([upper, jnp.ones((s, s), BF16)], axis=1)


def _sb_chunk(qs, ks, vs, w2, mask, st):
    s = w2.shape[0]
    r = qs[0].shape[0]
    n = ks[0].shape[0] // s
    zs, sps, pieces = [], [], []
    for q, k in zip(qs, ks):
        z = lax.dot_general(q, k, NT, preferred_element_type=F32)
        sp = _softplus(z)
        lf = -sp if mask is None else jnp.where(mask, -sp, 0.0)
        zs.append(z)
        sps.append(sp)
        pieces += [lf[:, i * s:(i + 1) * s] for i in range(n)]
    lf_rows = jnp.concatenate(pieces, axis=0)
    hi = lf_rows.astype(BF16)
    lo = (lf_rows - hi.astype(F32)).astype(BF16)
    half = lf_rows.shape[0]
    sums = jnp.dot(jnp.concatenate([hi, lo], axis=0), w2, preferred_element_type=F32)
    sums = sums[:half] + sums[half:]
    out = []
    for h in range(len(qs)):
        acc, carry = st[2 * h], st[2 * h + 1]
        afters = []
        for i in range(n):
            blk = sums[(h * n + i) * r:(h * n + i + 1) * r]
            afters.append(blk[:, :s] + carry)
            carry = carry + blk[:, s:]
        after = afters[0] if n == 1 else jnp.concatenate(afters, axis=1)
        w = jnp.exp(zs[h] - sps[h] + after)
        if mask is not None:
            w = jnp.where(mask, w, 0.0)
        acc = acc + jnp.dot(w.astype(BF16), vs[h], preferred_element_type=F32)
        out += [acc, carry]
    return tuple(out)


def _sb_prompt_kernel(q_ref, k_ref, v_ref, w2_ref, o_ref, *, tq, tk):
    i = pl.program_id(2)
    q = jnp.concatenate([q_ref[:, g * SB_HEAD_DIM:(g + 1) * SB_HEAD_DIM] for g in range(SB_GROUP)], axis=0)
    q = (q * SB_SCALE).astype(BF16)
    r = SB_GROUP * tq
    qpos = i * tq + lax.broadcasted_iota(I32, (r, tk), 0) % tq
    lane = lax.broadcasted_iota(I32, (r, tk), 1)
    w2 = w2_ref[...]
    n_all = (i * tq + tq - 1) // tk + 1
    n_full = (i * tq) // tk

    def block(j, st, masked):
        start = pl.multiple_of(j * tk, tk)
        k = k_ref[pl.ds(start, tk), :].astype(BF16)
        v = v_ref[pl.ds(start, tk), :].astype(BF16)
        mask = ((j * tk + lane) < qpos) if masked else None
        return _sb_chunk([q], [k], [v], w2, mask, st)

    st = (jnp.zeros((r, SB_HEAD_DIM), F32), jnp.zeros((r, tk), F32))
    st = lax.fori_loop(0, n_all - n_full, lambda it, st: block(n_all - 1 - it, st, True), st)
    acc, _ = lax.fori_loop(0, n_full, lambda it, st: block(n_full - 1 - it, st, False), st)
    for g in range(SB_GROUP):
        o_ref[:, g * SB_HEAD_DIM:(g + 1) * SB_HEAD_DIM] = acc[g * tq:(g + 1) * tq].astype(o_ref.dtype)


def sb_prompt(hmat, batch, seq, tq, tk):
    nq = seq // tq
    gw = SB_GROUP * SB_HEAD_DIM
    k_col0 = SB_HEADS * SB_HEAD_DIM // SB_HEAD_DIM
    v_col0 = k_col0 + SB_KV_HEADS
    return pl.pallas_call(
        functools.partial(_sb_prompt_kernel, tq=tq, tk=tk), grid=(batch, SB_KV_HEADS, nq),
        in_specs=[pl.BlockSpec((tq, gw), lambda b, kv, i: (b * nq + i, kv)),
                  pl.BlockSpec((seq, SB_HEAD_DIM), lambda b, kv, i: (b, k_col0 + kv)),
                  pl.BlockSpec((seq, SB_HEAD_DIM), lambda b, kv, i: (b, v_col0 + kv)),
                  pl.BlockSpec((tk, 2 * tk), lambda b, kv, i: (0, 0))],
        out_specs=pl.BlockSpec((tq, gw), lambda b, kv, i: (b * nq + i, kv)),
        out_shape=jax.ShapeDtypeStruct((batch * seq, SB_HEADS * SB_HEAD_DIM), BF16),
        compiler_params=_params(("parallel", "parallel", "arbitrary")))(hmat, hmat, hmat, _tri_sum_matrix(tk))


def _paged_pipeline(pt_ref, n_pages, ppc, copies_for, compute_chunk, state):
    b = pl.program_id(0)
    nb = pl.num_programs(0)
    nc = n_pages // ppc

    def start_chunk(bb, c, slot):
        for i in range(ppc):
            for cp in copies_for(pt_ref[bb * n_pages + c * ppc + i], slot, i):
                cp.start()

    def wait_chunk(slot):
        for i in range(ppc):
            for cp in copies_for(0, slot, i):
                cp.wait()

    @pl.when(b == 0)
    def _():
        start_chunk(0, 0, 0)

    def body(c, st):
        slot = (b * nc + c) & 1
        more = c + 1 < nc

        @pl.when(more | (b + 1 < nb))
        def _():
            start_chunk(jnp.where(more, b, b + 1), jnp.where(more, c + 1, 0), 1 - slot)

        wait_chunk(slot)
        return compute_chunk(slot, st)

    return lax.fori_loop(0, nc, body, state)


def _sb_sample_kernel(pt_ref, q_ref, kn_ref, vn_ref, w2_ref, ck_hbm, cv_hbm, o_ref, kbuf, vbuf, sem, *,
                      n_pages, ppc, n_new):
    dh = SB_HEAD_DIM
    rows = n_new * SB_GROUP
    page_rows = SB_KV_HEADS * PAGE_SIZE
    q = (q_ref[0] * SB_SCALE).astype(BF16)
    qs = [q[h] for h in range(SB_KV_HEADS)]
    w2 = w2_ref[...]

    def heads(ref, start, n_keys):
        return [ref[pl.ds(start + h, n_keys, stride=SB_KV_HEADS), :].astype(BF16) for h in range(SB_KV_HEADS)]

    tpos = lax.broadcasted_iota(I32, (rows, PAGE_SIZE), 0) // SB_GROUP
    mask = lax.broadcasted_iota(I32, (rows, PAGE_SIZE), 1) < tpos
    zero = (jnp.zeros((rows, dh), F32), jnp.zeros((rows, PAGE_SIZE), F32))
    st = _sb_chunk(qs, heads(kn_ref.at[0], 0, PAGE_SIZE), heads(vn_ref.at[0], 0, PAGE_SIZE), w2, mask,
                   zero * SB_KV_HEADS)

    def copies_for(page_id, slot, i):
        dst = pl.ds((slot * ppc + i) * page_rows, page_rows)
        return (pltpu.make_async_copy(ck_hbm.at[page_id], kbuf.at[dst], sem.at[0, slot]),
                pltpu.make_async_copy(cv_hbm.at[page_id], vbuf.at[dst], sem.at[1, slot]))

    def compute_chunk(slot, st):
        start = slot * ppc * page_rows
        n_keys = ppc * PAGE_SIZE
        return _sb_chunk(qs, heads(kbuf, start, n_keys), heads(vbuf, start, n_keys), w2, None, st)

    st = _paged_pipeline(pt_ref, n_pages, ppc, copies_for, compute_chunk, st)
    for h in range(SB_KV_HEADS):
        o_ref[0, h] = st[2 * h]


def sb_sample(pt_rev, q, k_new, v_new, cache_k, cache_v, ppc):
    db, kvh, rows, dh = q.shape
    n_pages = pt_rev.shape[0] // db
    page_rows = kvh * PAGE_SIZE
    grid_spec = pltpu.PrefetchScalarGridSpec(
        num_scalar_prefetch=1, grid=(db,),
        in_specs=[pl.BlockSpec((1, kvh, rows, dh), lambda b, pt: (b, 0, 0, 0)),
                  pl.BlockSpec((1, page_rows, dh), lambda b, pt: (b, 0, 0)),
                  pl.BlockSpec((1, page_rows, dh), lambda b, pt: (b, 0, 0)),
                  pl.BlockSpec((PAGE_SIZE, 2 * PAGE_SIZE), lambda b, pt: (0, 0)),
                  pl.BlockSpec(memory_space=pl.ANY), pl.BlockSpec(memory_space=pl.ANY)],
        out_specs=pl.BlockSpec((1, kvh, rows, dh), lambda b, pt: (b, 0, 0, 0)),
        scratch_shapes=[pltpu.VMEM((2 * ppc * page_rows, dh), F32), pltpu.VMEM((2 * ppc * page_rows, dh), F32),
                        pltpu.SemaphoreType.DMA((2, 2))])
    return pl.pallas_call(
        functools.partial(_sb_sample_kernel, n_pages=n_pages, ppc=ppc, n_new=rows // SB_GROUP),
        grid_spec=grid_spec, out_shape=jax.ShapeDtypeStruct(q.shape, F32),
        compiler_params=_params(("arbitrary",)))(pt_rev, q, k_new, v_new, _tri_sum_matrix(PAGE_SIZE), cache_k, cache_v)


def _softmax_update(s, v, m, l, acc):
    m_new = jnp.maximum(m, jnp.max(s, axis=-1, keepdims=True))
    alpha = jnp.exp(m - m_new)
    p = jnp.exp(s - m_new)
    l = l * alpha + jnp.sum(p, axis=-1, keepdims=True)
    acc = acc * alpha + jnp.dot(p.astype(BF16), v, preferred_element_type=F32)
    return m_new, l, acc


def _mla_prompt_kernel(q_ref, k_ref, wuv_ref, o_ref, qs_ref, acc_ref, m_ref, l_ref, *, tq, tk):
    i = pl.program_id(1)
    rows = MLA_HEADS * tq
    for h in range(MLA_HEADS):
        qs_ref[h * tq:(h + 1) * tq, :] = q_ref[:, h * QK_CAT:(h + 1) * QK_CAT]
    acc_ref[...] = jnp.zeros(acc_ref.shape, F32)
    m_ref[...] = jnp.full(m_ref.shape, NEG, F32)
    l_ref[...] = jnp.zeros(l_ref.shape, F32)

    def step(j, masked):
        k = k_ref[pl.ds(pl.multiple_of(j * tk, tk), tk), :]
        s = lax.dot_general(qs_ref[...], k, NT, preferred_element_type=F32)
        if masked:
            qpos = i * tq + lax.broadcasted_iota(I32, (rows, tk), 0) % tq
            s = jnp.where(j * tk + lax.broadcasted_iota(I32, (rows, tk), 1) <= qpos, s, NEG)
        m_prev = m_ref[...]
        m_new = jnp.maximum(m_prev, jnp.max(s, axis=-1, keepdims=True))
        alpha = jnp.exp(m_prev - m_new)
        p = jnp.exp(s - _lanes(m_new, tk))
        l_ref[...] = alpha * l_ref[...] + jnp.sum(p, axis=-1, keepdims=True)
        m_ref[...] = m_new
        acc_ref[...] = acc_ref[...] * _lanes(alpha, KV_LORA) + jnp.dot(p.astype(BF16), k[:, :KV_LORA],
                                                                    preferred_element_type=F32)

    n_full = (i * tq + 1) // tk
    n_all = (i * tq + tq - 1) // tk + 1

    def full_body(j, c):
        step(j, False)
        return c

    def edge_body(j, c):
        step(j, True)
        return c

    lax.fori_loop(0, n_full, full_body, 0)
    lax.fori_loop(n_full, n_all, edge_body, 0)
    for h in range(MLA_HEADS):
        sl = slice(h * tq, (h + 1) * tq)
        lat = (acc_ref[sl, :] / _lanes(l_ref[sl, :], KV_LORA)).astype(BF16)
        o_ref[:, h * V_DIM:(h + 1) * V_DIM] = jnp.dot(lat, wuv_ref[h], preferred_element_type=F32).astype(o_ref.dtype)


def mla_prompt(qcat, kcat, wuv, batch, seq, tq, tk):
    nq = seq // tq
    rows = MLA_HEADS * tq
    return pl.pallas_call(
        functools.partial(_mla_prompt_kernel, tq=tq, tk=tk), grid=(batch, nq),
        in_specs=[pl.BlockSpec((tq, MLA_HEADS * QK_CAT), lambda b, i: (b * nq + i, 0)),
                  pl.BlockSpec((seq, QK_CAT), lambda b, i: (b, 0)),
                  pl.BlockSpec(wuv.shape, lambda b, i: (0, 0, 0))],
        out_specs=pl.BlockSpec((tq, MLA_HEADS * V_DIM), lambda b, i: (b * nq + i, 0)),
        out_shape=jax.ShapeDtypeStruct((batch * seq, MLA_HEADS * V_DIM), BF16),
        scratch_shapes=[pltpu.VMEM((rows, QK_CAT), BF16), pltpu.VMEM((rows, KV_LORA), F32),
                        pltpu.VMEM((rows, LANES), F32), pltpu.VMEM((rows, LANES), F32)],
        compiler_params=_params(("parallel", "arbitrary")))(qcat, kcat, wuv)


def _mla_sample_kernel(pt_ref, q_ref, kn_ref, c_hbm, r_hbm, o_ref, cbuf, rbuf, sem, *, n_pages, ppc, n_new):
    rows = n_new * MLA_HEADS
    q = q_ref[0]
    q_lat, q_rope = q[:, :KV_LORA], q[:, KV_LORA:]
    kn = kn_ref[0]
    tpos = lax.broadcasted_iota(I32, (rows, PAGE_SIZE), 0) // MLA_HEADS
    mask = lax.broadcasted_iota(I32, (rows, PAGE_SIZE), 1) <= tpos
    s_new = jnp.where(mask, lax.dot_general(q, kn, NT, preferred_element_type=F32), NEG)
    init = (jnp.full((rows, 1), NEG, F32), jnp.zeros((rows, 1), F32), jnp.zeros((rows, KV_LORA), F32))
    st = _softmax_update(s_new, kn[:, :KV_LORA], *init)

    def copies_for(page_id, slot, i):
        return (pltpu.make_async_copy(c_hbm.at[page_id], cbuf.at[slot, pl.ds(i * PAGE_SIZE, PAGE_SIZE)], sem.at[0, slot]),
                pltpu.make_async_copy(r_hbm.at[page_id], rbuf.at[slot, i], sem.at[1, slot]))

    n_keys = ppc * PAGE_SIZE
    rope_pad = jnp.zeros((LANES - ROPE, n_keys), BF16)

    def compute_chunk(slot, st):
        c = cbuf[slot].astype(BF16)
        krt = jnp.concatenate([rbuf[slot, i] for i in range(ppc)], axis=1).astype(BF16)
        krt = jnp.concatenate([krt, rope_pad], axis=0)
        s = (lax.dot_general(q_lat, c, NT, preferred_element_type=F32)
             + jnp.dot(q_rope, krt, preferred_element_type=F32))
        return _softmax_update(s, c, *st)

    _, l, acc = _paged_pipeline(pt_ref, n_pages, ppc, copies_for, compute_chunk, st)
    o_ref[0] = (acc / l).astype(o_ref.dtype)


def mla_sample(pt, q, k_new, cache_ckv, cache_krt, ppc):
    db, rows, _ = q.shape
    n_pages = pt.shape[0] // db
    grid_spec = pltpu.PrefetchScalarGridSpec(
        num_scalar_prefetch=1, grid=(db,),
        in_specs=[pl.BlockSpec((1, rows, QK_CAT), lambda b, pt: (b, 0, 0)),
                  pl.BlockSpec((1, PAGE_SIZE, QK_CAT), lambda b, pt: (b, 0, 0)),
                  pl.BlockSpec(memory_space=pl.ANY), pl.BlockSpec(memory_space=pl.ANY)],
        out_specs=pl.BlockSpec((1, rows, KV_LORA), lambda b, pt: (b, 0, 0)),
        scratch_shapes=[pltpu.VMEM((2, ppc * PAGE_SIZE, KV_LORA), F32), pltpu.VMEM((2, ppc, ROPE, PAGE_SIZE), F32),
                        pltpu.SemaphoreType.DMA((2, 2))])
    return pl.pallas_call(
        functools.partial(_mla_sample_kernel, n_pages=n_pages, ppc=ppc, n_new=rows // MLA_HEADS),
        grid_spec=grid_spec, out_shape=jax.ShapeDtypeStruct((db, rows, KV_LORA), BF16),
        compiler_params=_params(("arbitrary",)))(pt, q, k_new, cache_ckv, cache_krt)


def _head_mm_kernel(a_ref, w_ref, o_ref):
    o_ref[...] = jnp.dot(a_ref[...], w_ref[0], preferred_element_type=F32).astype(o_ref.dtype)


def head_matmul(a, w, out_dtype):
    m = a.shape[0]
    h, k, n = w.shape
    return pl.pallas_call(
        _head_mm_kernel, grid=(h,),
        in_specs=[pl.BlockSpec((m, k), lambda i: (0, i)), pl.BlockSpec((1, k, n), lambda i: (i, 0, 0))],
        out_specs=pl.BlockSpec((m, n), lambda i: (0, i)),
        out_shape=jax.ShapeDtypeStruct((m, h * n), out_dtype),
        compiler_params=_params(("parallel",)))(a, w)


MEM_PAIR = 2


def _mem_attend(q, k, v):
    s = lax.dot_general((q * MEM_SCALE).astype(BF16), k.astype(BF16), NT, preferred_element_type=F32)
    p = jnp.exp(s - jnp.max(s, axis=-1, keepdims=True))
    o = jnp.dot(p.astype(BF16), v.astype(BF16), preferred_element_type=F32)
    return o / jnp.sum(p, axis=-1, keepdims=True)


def _mem_prompt_kernel(q_ref, k_ref, v_ref, o_ref):
    for h in range(MEM_PAIR):
        sl = slice(h * MEM_DIM, (h + 1) * MEM_DIM)
        o_ref[:, sl] = _mem_attend(q_ref[:, sl], k_ref[:, sl], v_ref[:, sl]).astype(o_ref.dtype)


def mem_attn_prompt(hmat, memkv, q_col0, batch, seq, n_mem, tq):
    nq = seq // tq
    w = MEM_PAIR * MEM_DIM
    n_pair = MEM_HEADS // MEM_PAIR
    return pl.pallas_call(
        _mem_prompt_kernel, grid=(batch, n_pair, nq),
        in_specs=[pl.BlockSpec((tq, w), lambda b, hp, i: (b * nq + i, q_col0 + hp)),
                  pl.BlockSpec((n_mem, w), lambda b, hp, i: (b, hp)),
                  pl.BlockSpec((n_mem, w), lambda b, hp, i: (b, n_pair + hp))],
        out_specs=pl.BlockSpec((tq, w), lambda b, hp, i: (b * nq + i, hp)),
        out_shape=jax.ShapeDtypeStruct((batch * seq, MEM_HEADS * MEM_DIM), BF16),
        compiler_params=_params(("parallel", "parallel", "arbitrary")))(hmat, memkv, memkv)


def _mem_sample_kernel(qa_ref, qb_ref, k_ref, v_ref, o_ref, *, n_new, group, n_mem):
    parts = MEM_DIM // LANES
    stride = MEM_HEADS * parts
    q = jnp.concatenate([qa_ref[...], qb_ref[...]], axis=-1)
    row = lax.broadcasted_iota(I32, (q.shape[0], MEM_DIM), 0) // n_new

    def head(ref, g, h):
        return jnp.concatenate([ref[g, pl.ds(p * MEM_HEADS + h, n_mem, stride=stride), :] for p in range(parts)], axis=-1)

    for h in range(MEM_HEADS):
        qh = q[:, h * MEM_DIM:(h + 1) * MEM_DIM]
        out = jnp.zeros(qh.shape, F32)
        for g in range(group):
            out = jnp.where(row == g, _mem_attend(qh, head(k_ref, g, h), head(v_ref, g, h)), out)
        o_ref[:, h * MEM_DIM:(h + 1) * MEM_DIM] = out.astype(o_ref.dtype)


def mem_sample(hmat, cache_k, cache_v, q_col0, row0, n_new, n_mem):
    db, cache_rows, _ = cache_k.shape
    group = 8 // n_new
    rows = group * n_new
    w = MEM_PAIR * MEM_DIM
    r0 = row0 // rows
    return pl.pallas_call(
        functools.partial(_mem_sample_kernel, n_new=n_new, group=group, n_mem=n_mem), grid=(db // group,),
        in_specs=[pl.BlockSpec((rows, w), lambda i: (r0 + i, q_col0)),
                  pl.BlockSpec((rows, w), lambda i: (r0 + i, q_col0 + 1)),
                  pl.BlockSpec((group, cache_rows, LANES), lambda i: (i, 0, 0)),
                  pl.BlockSpec((group, cache_rows, LANES), lambda i: (i, 0, 0))],
        out_specs=pl.BlockSpec((rows, MEM_HEADS * MEM_DIM), lambda i: (i, 0)),
        out_shape=jax.ShapeDtypeStruct((db * n_new, MEM_HEADS * MEM_DIM), BF16),
        compiler_params=_params(("arbitrary",)))(hmat, hmat, cache_k, cache_v)


def _mix_kernel(osb_ref, omla_ref, omem_ref, wsb_ref, wmla_ref, wmem_ref, gsb_ref, gmla_ref, gmem_ref, o_ref):
    mix = jnp.zeros(o_ref.shape, F32)
    for o, w, g in ((osb_ref, wsb_ref, gsb_ref), (omla_ref, wmla_ref, gmla_ref), (omem_ref, wmem_ref, gmem_ref)):
        mix = mix + jax.nn.sigmoid(g[...]) * jnp.dot(o[...], w[...], preferred_element_type=F32)
    o_ref[...] = mix.astype(o_ref.dtype)


def branch_mix(o_sb, o_mla, o_mem, w_sb, w_mla, w_mem, hmat, gate_col0, d, tm, tn):
    t, kdim = o_sb.shape
    nn = d // tn
    a_spec = pl.BlockSpec((tm, kdim), lambda i, j: (i, 0))
    w_spec = pl.BlockSpec((kdim, tn), lambda i, j: (0, j))
    g_specs = [pl.BlockSpec((tm, tn), lambda i, j, br=br: (i, gate_col0 + br * nn + j)) for br in range(3)]
    return pl.pallas_call(
        _mix_kernel, grid=(t // tm, nn),
        in_specs=[a_spec, a_spec, a_spec, w_spec, w_spec, w_spec] + g_specs,
        out_specs=pl.BlockSpec((tm, tn), lambda i, j: (i, j)),
        out_shape=jax.ShapeDtypeStruct((t, d), BF16),
        compiler_params=_params(("parallel", "arbitrary")))(o_sb, o_mla, o_mem, w_sb, w_mla, w_mem, hmat, hmat, hmat)


def _out_proj_kernel(mix_ref, w_ref, x_ref, g_ref, rhi_ref, rlo_ref, b_ref, xmid_ref, hn_ref, lg_ref):
    xmid = x_ref[...] + jnp.dot(mix_ref[...], w_ref[...], preferred_element_type=F32)
    xmid_ref[...] = xmid
    hn = _rms(xmid, g_ref[...])
    hn_ref[...] = hn
    hi = hn.astype(BF16)
    lo = (hn - hi.astype(F32)).astype(BF16)
    lg = (jnp.dot(hi, rhi_ref[...], preferred_element_type=F32) + jnp.dot(lo, rhi_ref[...], preferred_element_type=F32)
          + jnp.dot(hi, rlo_ref[...], preferred_element_type=F32))
    lg_ref[...] = lg + b_ref[...]


def out_proj(mix, w_out, x, g_ffn, r_hi, r_lo, r_bias, tm):
    t, d = x.shape
    row = lambda i: (i, 0)
    fix = lambda i: (0, 0)
    return pl.pallas_call(
        _out_proj_kernel, grid=(t // tm,),
        in_specs=[pl.BlockSpec((tm, d), row), pl.BlockSpec((d, d), fix), pl.BlockSpec((tm, d), row),
                  pl.BlockSpec((1, d), fix), pl.BlockSpec((d, LANES), fix), pl.BlockSpec((d, LANES), fix),
                  pl.BlockSpec((1, LANES), fix)],
        out_specs=[pl.BlockSpec((tm, d), row), pl.BlockSpec((tm, d), row), pl.BlockSpec((tm, LANES), row)],
        out_shape=[jax.ShapeDtypeStruct((t, d), F32), jax.ShapeDtypeStruct((t, d), F32),
                   jax.ShapeDtypeStruct((t, LANES), F32)],
        compiler_params=_params(("parallel",)))(mix, w_out, x, g_ffn.reshape(1, d), r_hi, r_lo, r_bias)


def _route_kernel(lg_ref, o_ref):
    lg = lg_ref[...]
    lane = lax.broadcasted_iota(I32, lg.shape, 1)
    lanef = lane.astype(F32)

    def first_max(x):
        v = jnp.max(x, axis=-1, keepdims=True)
        return v, jnp.min(jnp.where(x == v, lanef, 1e6), axis=-1, keepdims=True)

    is_grp = lane < N_GROUPS
    gl = jnp.where(is_grp, lg, NEG)
    gmax, g_idx = first_max(gl)
    g_p = 1.0 / jnp.sum(jnp.where(is_grp, jnp.exp(gl - gmax), 0.0), axis=-1, keepdims=True)
    grp_of_lane = ((lane - N_GROUPS) >> 3).astype(F32)
    el = jnp.where((lane >= N_GROUPS) & (grp_of_lane == g_idx), lg, NEG)
    v1, i1 = first_max(el)
    v2, i2 = first_max(jnp.where(lanef == i1, NEG, el))
    d = jnp.exp(v2 - v1)
    w1 = g_p / (1.0 + d)
    w2 = g_p * d / (1.0 + d)
    e1 = i1 - N_GROUPS
    e2 = i2 - N_GROUPS
    o_ref[...] = jnp.where(lane == 0, e1, jnp.where(lane == 1, e2, jnp.where(lane == 2, w1, jnp.where(lane == 3, w2, 0.0))))


def route(logits, tm):
    t = logits.shape[0]
    return pl.pallas_call(
        _route_kernel, grid=(t // tm,),
        in_specs=[pl.BlockSpec((tm, LANES), lambda i: (i, 0))], out_specs=pl.BlockSpec((tm, LANES), lambda i: (i, 0)),
        out_shape=jax.ShapeDtypeStruct((t, LANES), F32), compiler_params=_params(("parallel",)))(logits)


def _slab_copies(idx_ref, base, n, src_hbm, dst, dst_row0, spr, sem):
    cps = []
    for r in range(n):
        src_row = pl.multiple_of(idx_ref[base + r] * spr, spr)
        cps.append(pltpu.make_async_copy(src_hbm.at[pl.ds(src_row, spr)], dst.at[pl.ds(dst_row0 + r * spr, spr)], sem))
    return cps


def _expert_kernel(tok_ref, be_ref, nused_ref, h_hbm, wg_ref, wu_ref, wd_ref, o_ref, xbuf, sem, *, spr):
    u = pl.program_id(0)
    n_used = nused_ref[0]
    blk_rows = MOE_BLOCK * spr

    def gather(blk, slot):
        return _slab_copies(tok_ref, blk * MOE_BLOCK, MOE_BLOCK, h_hbm, xbuf, slot * blk_rows, spr, sem.at[slot])

    @pl.when(u == 0)
    def _():
        for cp in gather(0, 0):
            cp.start()

    @pl.when(u + 1 < n_used)
    def _():
        for cp in gather(u + 1, (u + 1) & 1):
            cp.start()

    @pl.when(u < n_used)
    def _():
        slot = u & 1
        for cp in gather(u, slot):
            cp.wait()
        x = jnp.concatenate([xbuf[pl.ds(slot * blk_rows + c, MOE_BLOCK, stride=spr), :] for c in range(spr)], axis=-1)
        x = x.astype(BF16)
        gate = jnp.dot(x, wg_ref[0].astype(BF16), preferred_element_type=F32)
        up = jnp.dot(x, wu_ref[0].astype(BF16), preferred_element_type=F32)
        a = (gate * jax.nn.sigmoid(gate) * up).astype(BF16)
        y = jnp.dot(a, wd_ref[0].astype(BF16), preferred_element_type=F32)
        for c in range(spr):
            o_ref[pl.ds(c, MOE_BLOCK, stride=spr), :] = y[:, c * LANES:(c + 1) * LANES]

    @pl.when(u >= n_used)
    def _():
        o_ref[...] = jnp.zeros(o_ref.shape, o_ref.dtype)


def moe_experts(slot_tok, block_e, n_used, h_slabs, w_gate, w_up, w_down):
    n_slots = slot_tok.shape[0]
    _, d, de = w_gate.shape
    spr = d // LANES
    wsel = lambda u, tok, be, nu: (be[jnp.minimum(u, nu[0] - 1)], 0, 0)
    grid_spec = pltpu.PrefetchScalarGridSpec(
        num_scalar_prefetch=3, grid=(n_slots // MOE_BLOCK,),
        in_specs=[pl.BlockSpec(memory_space=pl.ANY), pl.BlockSpec((1, d, de), wsel), pl.BlockSpec((1, d, de), wsel),
                  pl.BlockSpec((1, de, d), wsel)],
        out_specs=pl.BlockSpec((MOE_BLOCK * spr, LANES), lambda u, tok, be, nu: (u, 0)),
        scratch_shapes=[pltpu.VMEM((2 * MOE_BLOCK * spr, LANES), F32), pltpu.SemaphoreType.DMA((2,))])
    return pl.pallas_call(functools.partial(_expert_kernel, spr=spr), grid_spec=grid_spec,
                          out_shape=jax.ShapeDtypeStruct((n_slots * spr, LANES), F32),
                          compiler_params=_params(("arbitrary",)))(slot_tok, block_e, n_used, h_slabs, w_gate, w_up, w_down)


def _combine_kernel(s1_ref, s2_ref, x_ref, rt_ref, g_ref, y_hbm, o_ref, ybuf, sem, *, tm, spr):
    i = pl.program_id(0)
    n = pl.num_programs(0)
    half_rows = tm * spr

    def gather(step, slot):
        base = step * tm
        return (_slab_copies(s1_ref, base, tm, y_hbm, ybuf, (2 * slot) * half_rows, spr, sem.at[slot])
                + _slab_copies(s2_ref, base, tm, y_hbm, ybuf, (2 * slot + 1) * half_rows, spr, sem.at[slot]))

    @pl.when(i == 0)
    def _():
        for cp in gather(0, 0):
            cp.start()

    @pl.when(i + 1 < n)
    def _():
        for cp in gather(i + 1, (i + 1) & 1):
            cp.start()

    slot = i & 1
    for cp in gather(i, slot):
        cp.wait()
    rt = rt_ref[...]
    w1, w2 = rt[:, 2:3], rt[:, 3:4]
    ss = jnp.zeros((tm, 1), F32)
    for c in range(spr):
        sl = slice(c * LANES, (c + 1) * LANES)
        y1 = ybuf[pl.ds((2 * slot) * half_rows + c, tm, stride=spr), :]
        y2 = ybuf[pl.ds((2 * slot + 1) * half_rows + c, tm, stride=spr), :]
        xc = x_ref[:, sl] + (w1 * y1 + w2 * y2)
        ss = ss + jnp.sum(xc * xc, axis=-1, keepdims=True)
        o_ref[:, sl] = xc
    scale = lax.rsqrt(ss / (spr * LANES) + NORM_EPS)
    o_ref[...] = o_ref[...] * scale * g_ref[...]


def moe_combine(slot1, slot2, xmid, routed, g_final, y_slabs, tm):
    t, d = xmid.shape
    spr = d // LANES
    grid_spec = pltpu.PrefetchScalarGridSpec(
        num_scalar_prefetch=2, grid=(t // tm,),
        in_specs=[pl.BlockSpec((tm, d), lambda i, a, b: (i, 0)), pl.BlockSpec((tm, LANES), lambda i, a, b: (i, 0)),
                  pl.BlockSpec((1, d), lambda i, a, b: (0, 0)), pl.BlockSpec(memory_space=pl.ANY)],
        out_specs=pl.BlockSpec((tm, d), lambda i, a, b: (i, 0)),
        scratch_shapes=[pltpu.VMEM((4 * tm * spr, LANES), F32), pltpu.SemaphoreType.DMA((2,))])
    return pl.pallas_call(functools.partial(_combine_kernel, tm=tm, spr=spr), grid_spec=grid_spec,
                          out_shape=jax.ShapeDtypeStruct((t, d), F32),
                          compiler_params=_params(("arbitrary",)))(slot1, slot2, xmid, routed, g_final.reshape(1, d), y_slabs)


def _dispatch_plan(e12, n_exp):
    t = e12.shape[0]
    flat_e = e12.reshape(-1)
    tk = flat_e.shape[0]
    n_blocks = -(-tk // MOE_BLOCK) + n_exp
    order = jnp.argsort(flat_e).astype(I32)
    se = flat_e[order]
    counts = jnp.bincount(flat_e, length=n_exp).astype(I32)
    start = jnp.cumsum(counts) - counts
    padded = (counts + MOE_BLOCK - 1) // MOE_BLOCK * MOE_BLOCK
    pad_end = jnp.cumsum(padded)
    dest = (pad_end - padded)[se] + jnp.arange(tk, dtype=I32) - start[se]
    slot_of = jnp.zeros((tk,), I32).at[order].set(dest).reshape(t, 2)
    slot_tok = jnp.zeros((n_blocks * MOE_BLOCK,), I32).at[dest].set(order // 2)
    block_e = jnp.minimum(jnp.searchsorted(pad_end, jnp.arange(n_blocks, dtype=I32) * MOE_BLOCK, side='right'),
                          n_exp - 1).astype(I32)
    n_used = (pad_end[-1:] // MOE_BLOCK).astype(I32)
    return slot_of, slot_tok, block_e, n_used


def _tile(n, target, mult):
    best = None
    for c in range(mult, min(n, target) + 1, mult):
        if n % c == 0:
            best = c
    assert best is not None, (n, target, mult)
    return best


def kernel(x_prompt, x_sample, mem_prompt, cache_sb_k, cache_sb_v, cache_mla_ckv, cache_mla_krope, cache_mem_k, cache_mem_v, page_table, g_attn_norm, w_in, g_q_norm, w_uq, g_kv_norm, w_uk, w_uv, g_mem_norm, w_mem_kv, w_br_sb, w_br_mla, w_br_mem, w_out, g_ffn_norm, w_grp, b_grp, w_rtr, b_rtr, w_gate, w_up, w_down, g_final):
    depth = g_attn_norm.shape[0]
    assert depth == 1
    batch, seq, d = x_prompt.shape
    db, n_new, _ = x_sample.shape
    n_mem = mem_prompt.shape[1]
    n_pages = page_table.shape[1]
    n_pool = cache_sb_k.shape[1]
    tp, ts = batch * seq, db * n_new
    t = tp + ts
    l = 0

    sbq_w, sbkv_w = SB_HEADS * SB_HEAD_DIM, SB_KV_HEADS * SB_HEAD_DIM
    memq_w = MEM_HEADS * MEM_DIM
    c_kr = 2 * sbkv_w + sbq_w + Q_LORA + KV_LORA
    w = w_in[l]
    wp = jnp.concatenate([w[:, :c_kr], w[:, c_kr + ROPE:], w[:, c_kr:c_kr + ROPE], jnp.zeros((d, LANES - ROPE), F32)],
                         axis=1).astype(BF16)
    col_cq = sbq_w + 2 * sbkv_w
    col_ckv = col_cq + Q_LORA
    col_memq = col_ckv + KV_LORA
    col_gate = col_memq + memq_w
    col_kr = col_gate + 3 * d
    n_cols = wp.shape[1]

    half = ROPE // 2
    inv = ROPE_THETA ** (-jnp.arange(half, dtype=F32) / half)
    pos = jnp.concatenate([jnp.tile(jnp.arange(seq), batch), jnp.tile(n_pages * PAGE_SIZE + jnp.arange(n_new), db)])
    ang = pos.astype(F32)[:, None] * inv[None, :]
    cos, sin = jnp.cos(ang), jnp.sin(ang)
    zeros = jnp.zeros_like(cos)
    cos_q, sin_q = jnp.tile(cos, (1, MLA_HEADS)), jnp.tile(sin, (1, MLA_HEADS))
    c_tab = jnp.concatenate([cos, cos, zeros, zeros], axis=1)
    s1_tab = jnp.concatenate([-sin, zeros, zeros, zeros], axis=1)
    s2_tab = jnp.concatenate([zeros, sin, zeros, zeros], axis=1)

    wuq = w_uq[l]
    wuq2 = jnp.concatenate([wuq[:, :, :NOPE].reshape(Q_LORA, -1), wuq[:, :, NOPE:NOPE + half].reshape(Q_LORA, -1),
                            wuq[:, :, NOPE + half:].reshape(Q_LORA, -1)], axis=1).astype(BF16)
    wuk_t = jnp.transpose(w_uk[l], (1, 2, 0)).astype(BF16)
    wuv = jnp.transpose(w_uv[l], (1, 0, 2)).astype(BF16)
    hh = jnp.arange(MLA_HEADS)[:, None]
    ii = jnp.arange(half)[None, :]
    sel = jnp.zeros((2 * MLA_HEADS * half, MLA_HEADS * LANES), BF16)
    sel = sel.at[(hh * half + ii).ravel(), (hh * LANES + ii).ravel()].set(1.0)
    sel = sel.at[(MLA_HEADS * half + hh * half + ii).ravel(), (hh * LANES + half + ii).ravel()].set(1.0)

    xcat = jnp.concatenate([x_prompt.reshape(tp, d), x_sample.reshape(ts, d)], axis=0)
    tm_big = _tile(t, 1152, 16)
    hn_attn = rmsnorm(xcat, g_attn_norm[l], BF16, _tile(t, 640, 16))
    hmat = matmul(hn_attn, wp, F32, tm_big, _tile(n_cols, 1024, LANES))

    tm_mid = _tile(t, 640, 16)
    qcat = mla_q(hmat, col_cq // Q_LORA, g_q_norm[l], wuq2, wuk_t, sel, cos_q, sin_q, tm_mid)
    ckv_n, kr_rot, kcat = mla_kv(hmat, col_ckv // KV_LORA, col_kr // LANES, g_kv_norm[l], c_tab, s1_tab, s2_tab, tm_mid)

    memn = rmsnorm(mem_prompt.reshape(batch * n_mem, d), g_mem_norm[l], BF16, _tile(batch * n_mem, 512, 16))
    memkv = matmul(memn, w_mem_kv[l].astype(BF16), F32, _tile(batch * n_mem, 1024, 16), 1024)

    o_sb_p = sb_prompt(hmat, batch, seq, 128, 256)
    o_mla_p = mla_prompt(qcat, kcat, wuv, batch, seq, 128, 256)
    o_mem_p = mem_attn_prompt(hmat, memkv, col_memq // (MEM_PAIR * MEM_DIM), batch, seq, n_mem, _tile(seq, 512, 8))

    hs = hmat[tp:]
    rows_sb = n_new * SB_GROUP
    q_sb = hs[:, :sbq_w].reshape(db, n_new, SB_KV_HEADS, SB_GROUP, SB_HEAD_DIM)
    q_sb = jnp.transpose(q_sb, (0, 2, 1, 3, 4)).reshape(db, SB_KV_HEADS, rows_sb, SB_HEAD_DIM)
    page_rows = PAGE_SIZE * SB_KV_HEADS

    def new_kv(cols):
        x = hs[:, cols:cols + sbkv_w].reshape(db, n_new * SB_KV_HEADS, SB_HEAD_DIM)
        return jnp.pad(x, ((0, 0), (0, page_rows - n_new * SB_KV_HEADS), (0, 0)))

    pt = page_table.astype(I32)
    ppc = _tile(n_pages, 16, 1)
    o_sb_s = sb_sample(pt[:, ::-1].reshape(-1), q_sb, new_kv(sbq_w), new_kv(sbq_w + sbkv_w),
                       cache_sb_k[l].reshape(n_pool, page_rows, SB_HEAD_DIM),
                       cache_sb_v[l].reshape(n_pool, page_rows, SB_HEAD_DIM), ppc)
    o_sb_s = jnp.transpose(o_sb_s.reshape(db, SB_KV_HEADS, n_new, SB_GROUP, SB_HEAD_DIM), (0, 2, 1, 3, 4))
    o_sb_s = o_sb_s.reshape(ts, sbq_w).astype(BF16)

    q_mla_s = qcat[tp:].reshape(db, n_new * MLA_HEADS, QK_CAT)
    kcat_new = jnp.pad(kcat[tp:].reshape(db, n_new, QK_CAT), ((0, 0), (0, PAGE_SIZE - n_new), (0, 0)))
    lat_s = mla_sample(pt.reshape(-1), q_mla_s, kcat_new, cache_mla_ckv[l],
                       jnp.transpose(cache_mla_krope[l], (0, 2, 1)), ppc)
    o_mla_s = head_matmul(lat_s.reshape(ts, MLA_HEADS * KV_LORA), wuv, BF16)

    def mem_rows(c):
        c = c.reshape(db, n_mem, MEM_HEADS, MEM_DIM // LANES, LANES)
        return jnp.transpose(c, (0, 1, 3, 2, 4)).reshape(db, n_mem * memq_w // LANES, LANES)

    o_mem_s = mem_sample(hmat, mem_rows(cache_mem_k[l]), mem_rows(cache_mem_v[l]),
                         col_memq // (MEM_PAIR * MEM_DIM), tp, n_new, n_mem)

    o_sb = jnp.concatenate([o_sb_p, o_sb_s], axis=0)
    o_mla = jnp.concatenate([o_mla_p, o_mla_s], axis=0)
    o_mem = jnp.concatenate([o_mem_p, o_mem_s], axis=0)
    tn_mix = 512
    mix = branch_mix(o_sb, o_mla, o_mem, w_br_sb[l].astype(BF16), w_br_mla[l].astype(BF16), w_br_mem[l].astype(BF16),
                     hmat, col_gate // tn_mix, d, tm_big, tn_mix)
    w_r = jnp.concatenate([w_grp[l], w_rtr[l], jnp.zeros((d, LANES - N_GROUPS - N_EXPERTS), F32)], axis=1)
    r_hi = w_r.astype(BF16)
    r_lo = (w_r - r_hi.astype(F32)).astype(BF16)
    r_bias = jnp.concatenate([b_grp[l], b_rtr[l], jnp.zeros((LANES - N_GROUPS - N_EXPERTS,), F32)]).reshape(1, LANES)
    xmid, hn_ffn, logits = out_proj(mix, w_out[l].astype(BF16), xcat, g_ffn_norm[l], r_hi, r_lo, r_bias, _tile(t, 320, 16))

    routed = route(logits, _tile(t, 1152, 8))
    slot_of, slot_tok, block_e, n_used = _dispatch_plan(routed[:, :2].astype(I32), N_EXPERTS)
    spr = d // LANES
    y_slabs = moe_experts(slot_tok, block_e, n_used, hn_ffn.reshape(t * spr, LANES), w_gate[l], w_up[l], w_down[l])
    y = moe_combine(slot_of[:, 0], slot_of[:, 1], xmid, routed, g_final, y_slabs, _tile(t, 128, 8))

    def kv_out(cols, rows, shape):
        return rows[:, cols[0]:cols[1]].reshape(shape)[None]

    hp = hmat[:tp]
    return (y[:tp].reshape(batch, seq, d), y[tp:].reshape(db, n_new, d),
            kv_out((sbq_w, sbq_w + sbkv_w), hp, (batch, seq, SB_KV_HEADS, SB_HEAD_DIM)),
            kv_out((sbq_w + sbkv_w, sbq_w + 2 * sbkv_w), hp, (batch, seq, SB_KV_HEADS, SB_HEAD_DIM)),
            ckv_n[:tp].reshape(batch, seq, KV_LORA)[None], kr_rot[:tp].reshape(batch, seq, ROPE)[None],
            memkv[:, :memq_w].reshape(batch, n_mem, MEM_HEADS, MEM_DIM)[None],
            memkv[:, memq_w:].reshape(batch, n_mem, MEM_HEADS, MEM_DIM)[None],
            kv_out((sbq_w, sbq_w + sbkv_w), hs, (db, n_new, SB_KV_HEADS, SB_HEAD_DIM)),
            kv_out((sbq_w + sbkv_w, sbq_w + 2 * sbkv_w), hs, (db, n_new, SB_KV_HEADS, SB_HEAD_DIM)),
            ckv_n[tp:].reshape(db, n_new, KV_LORA)[None], kr_rot[tp:].reshape(db, n_new, ROPE)[None])
```

```python
import functools

import jax
import jax.numpy as jnp
from jax import lax
from jax.experimental import pallas as pl
from jax.experimental.pallas import tpu as pltpu

F32 = jnp.float32
BF16 = jnp.bfloat16
I32 = jnp.int32

NORM_EPS = 1e-6
ROPE_THETA = 10000.0
PAGE_SIZE = 128
LANES = 128
N_DMA_THREADS = 2
SB_HEADS, SB_KV_HEADS, SB_HEAD_DIM = 8, 2, 128
SB_GROUP = SB_HEADS // SB_KV_HEADS
SB_SCALE = SB_HEAD_DIM ** -0.5
MLA_HEADS, Q_LORA, KV_LORA, NOPE, ROPE, V_DIM = 8, 512, 512, 128, 64, 128
MLA_SCALE = (NOPE + ROPE) ** -0.5
QK_CAT = KV_LORA + LANES
MEM_HEADS, MEM_DIM = 4, 256
MEM_SCALE = MEM_DIM ** -0.5
N_GROUPS, EPG, N_EXPERTS, D_EXPERT = 8, 8, 64, 512
MOE_BLOCK = 128
NEG = -1e30
VMEM_LIMIT = 56 * 1024 * 1024
NT = (((1,), (1,)), ((), ()))


def _params(sem):
    return pltpu.CompilerParams(dimension_semantics=sem, vmem_limit_bytes=VMEM_LIMIT)


def _rms(x, g):
    return x * lax.rsqrt(jnp.mean(x * x, axis=-1, keepdims=True) + NORM_EPS) * g


def _lanes(x, n):
    return x if n == LANES else jnp.concatenate([x] * (n // LANES), axis=1)


def _rmsnorm_kernel(x_ref, g_ref, o_ref):
    o_ref[...] = _rms(x_ref[...], g_ref[...]).astype(o_ref.dtype)


def rmsnorm(x, g, out_dtype, tm):
    r, d = x.shape
    return pl.pallas_call(
        _rmsnorm_kernel, grid=(r // tm,),
        in_specs=[pl.BlockSpec((tm, d), lambda i: (i, 0)), pl.BlockSpec((1, d), lambda i: (0, 0))],
        out_specs=pl.BlockSpec((tm, d), lambda i: (i, 0)),
        out_shape=jax.ShapeDtypeStruct((r, d), out_dtype),
        compiler_params=_params(("parallel",)))(x, g.reshape(1, d))


def _mm_kernel(a_ref, b_ref, o_ref):
    o_ref[...] = jnp.dot(a_ref[...], b_ref[...], preferred_element_type=F32).astype(o_ref.dtype)


def matmul(a, b, out_dtype, tm, tn):
    m, k = a.shape
    n = b.shape[1]
    return pl.pallas_call(
        _mm_kernel, grid=(m // tm, n // tn),
        in_specs=[pl.BlockSpec((tm, k), lambda i, j: (i, 0)), pl.BlockSpec((k, tn), lambda i, j: (0, j))],
        out_specs=pl.BlockSpec((tm, tn), lambda i, j: (i, j)),
        out_shape=jax.ShapeDtypeStruct((m, n), out_dtype),
        compiler_params=_params(("parallel", "arbitrary")))(a, b)


def _mla_q_kernel(cq_ref, g_ref, wuq_ref, wuk_ref, sel_ref, cos_ref, sin_ref, o_ref):
    cqn = _rms(cq_ref[...], g_ref[...]).astype(BF16)
    q = jnp.dot(cqn, wuq_ref[...], preferred_element_type=F32) * MLA_SCALE
    n_np = MLA_HEADS * NOPE
    half = MLA_HEADS * (ROPE // 2)
    r1 = q[:, n_np:n_np + half]
    r2 = q[:, n_np + half:n_np + 2 * half]
    cos, sin = cos_ref[...], sin_ref[...]
    rot = jnp.concatenate([r1 * cos - r2 * sin, r1 * sin + r2 * cos], axis=-1).astype(BF16)
    tail = jnp.dot(rot, sel_ref[...], preferred_element_type=F32)
    for h in range(MLA_HEADS):
        nope = q[:, h * NOPE:(h + 1) * NOPE].astype(BF16)
        lat = jnp.dot(nope, wuk_ref[h], preferred_element_type=F32)
        o_ref[:, h * QK_CAT:h * QK_CAT + KV_LORA] = lat.astype(o_ref.dtype)
        o_ref[:, h * QK_CAT + KV_LORA:(h + 1) * QK_CAT] = tail[:, h * LANES:(h + 1) * LANES].astype(o_ref.dtype)


def mla_q(hmat, cq_col, g_q, wuq, wuk_t, sel, cos_q, sin_q, tm):
    t = hmat.shape[0]
    return pl.pallas_call(
        _mla_q_kernel, grid=(t // tm,),
        in_specs=[pl.BlockSpec((tm, Q_LORA), lambda i: (i, cq_col)),
                  pl.BlockSpec((1, Q_LORA), lambda i: (0, 0)),
                  pl.BlockSpec(wuq.shape, lambda i: (0, 0)),
                  pl.BlockSpec(wuk_t.shape, lambda i: (0, 0, 0)),
                  pl.BlockSpec(sel.shape, lambda i: (0, 0)),
                  pl.BlockSpec((tm, cos_q.shape[1]), lambda i: (i, 0)),
                  pl.BlockSpec((tm, sin_q.shape[1]), lambda i: (i, 0))],
        out_specs=pl.BlockSpec((tm, MLA_HEADS * QK_CAT), lambda i: (i, 0)),
        out_shape=jax.ShapeDtypeStruct((t, MLA_HEADS * QK_CAT), BF16),
        compiler_params=_params(("parallel",)))(hmat, g_q.reshape(1, -1), wuq, wuk_t, sel, cos_q, sin_q)


def _mla_kv_kernel(ckv_ref, kr_ref, g_ref, c_ref, s1_ref, s2_ref, ckv_o, kr_o, kcat_o):
    ckv_n = _rms(ckv_ref[...], g_ref[...])
    ckv_o[...] = ckv_n
    kr = kr_ref[...]
    rot = kr * c_ref[...] + pltpu.roll(kr, 96, 1) * s1_ref[...] + pltpu.roll(kr, 32, 1) * s2_ref[...]
    kr_o[...] = rot[:, :ROPE]
    kcat_o[:, :KV_LORA] = ckv_n.astype(BF16)
    kcat_o[:, KV_LORA:] = rot.astype(BF16)


def mla_kv(hmat, ckv_col, kr_col, g_kv, c_tab, s1_tab, s2_tab, tm):
    t = hmat.shape[0]
    row = lambda i: (i, 0)
    return pl.pallas_call(
        _mla_kv_kernel, grid=(t // tm,),
        in_specs=[pl.BlockSpec((tm, KV_LORA), lambda i: (i, ckv_col)),
                  pl.BlockSpec((tm, LANES), lambda i: (i, kr_col)),
                  pl.BlockSpec((1, KV_LORA), lambda i: (0, 0)),
                  pl.BlockSpec((tm, LANES), row), pl.BlockSpec((tm, LANES), row), pl.BlockSpec((tm, LANES), row)],
        out_specs=[pl.BlockSpec((tm, KV_LORA), row), pl.BlockSpec((tm, ROPE), row), pl.BlockSpec((tm, QK_CAT), row)],
        out_shape=[jax.ShapeDtypeStruct((t, KV_LORA), F32), jax.ShapeDtypeStruct((t, ROPE), F32),
                   jax.ShapeDtypeStruct((t, QK_CAT), BF16)],
        compiler_params=_params(("parallel",)))(hmat, hmat, g_kv.reshape(1, -1), c_tab, s1_tab, s2_tab)


def _softplus(z):
    return jnp.maximum(z, 0.0) + jnp.log(1.0 + jnp.exp(-jnp.abs(z)))


def _tri_sum_matrix(s):
    j = jnp.arange(s)
    upper = (j[:, None] > j[None, :]).astype(BF16)
    return jnp.concatenate([upper, jnp.ones((s, s), BF16)], axis=1)


def _sb_chunk(qs, ks, vs, w2, mask, st):
    s = w2.shape[0]
    r = qs[0].shape[0]
    n = ks[0].shape[0] // s
    zs, sps, pieces = [], [], []
    for q, k in zip(qs, ks):
        z = lax.dot_general(q, k, NT, preferred_element_type=F32)
        sp = _softplus(z)
        lf = -sp if mask is None else jnp.where(mask, -sp, 0.0)
        zs.append(z)
        sps.append(sp)
        pieces += [lf[:, i * s:(i + 1) * s] for i in range(n)]
    lf_rows = jnp.concatenate(pieces, axis=0)
    hi = lf_rows.astype(BF16)
    lo = (lf_rows - hi.astype(F32)).astype(BF16)
    half = lf_rows.shape[0]
    sums = jnp.dot(jnp.concatenate([hi, lo], axis=0), w2, preferred_element_type=F32)
    sums = sums[:half] + sums[half:]
    out = []
    for h in range(len(qs)):
        acc, carry = st[2 * h], st[2 * h + 1]
        afters = []
        for i in range(n):
            blk = sums[(h * n + i) * r:(h * n + i + 1) * r]
            afters.append(blk[:, :s] + carry)
            carry = carry + blk[:, s:]
        after = afters[0] if n == 1 else jnp.concatenate(afters, axis=1)
        w = jnp.exp(zs[h] - sps[h] + after)
        if mask is not None:
            w = jnp.where(mask, w, 0.0)
        acc = acc + jnp.dot(w.astype(BF16), vs[h], preferred_element_type=F32)
        out += [acc, carry]
    return tuple(out)


def _sb_prompt_kernel(q_ref, k_ref, v_ref, w2_ref, o_ref, *, tq, tk):
    i = pl.program_id(2)
    q = jnp.concatenate([q_ref[:, g * SB_HEAD_DIM:(g + 1) * SB_HEAD_DIM] for g in range(SB_GROUP)], axis=0)
    q = (q * SB_SCALE).astype(BF16)
    r = SB_GROUP * tq
    qpos = i * tq + lax.broadcasted_iota(I32, (r, tk), 0) % tq
    lane = lax.broadcasted_iota(I32, (r, tk), 1)
    w2 = w2_ref[...]
    n_all = (i * tq + tq - 1) // tk + 1
    n_full = (i * tq) // tk

    def block(j, st, masked):
        start = pl.multiple_of(j * tk, tk)
        k = k_ref[pl.ds(start, tk), :].astype(BF16)
        v = v_ref[pl.ds(start, tk), :].astype(BF16)
        mask = ((j * tk + lane) < qpos) if masked else None
        return _sb_chunk([q], [k], [v], w2, mask, st)

    st = (jnp.zeros((r, SB_HEAD_DIM), F32), jnp.zeros((r, tk), F32))
    st = lax.fori_loop(0, n_all - n_full, lambda it, st: block(n_all - 1 - it, st, True), st)
    acc, _ = lax.fori_loop(0, n_full, lambda it, st: block(n_full - 1 - it, st, False), st)
    for g in range(SB_GROUP):
        o_ref[:, g * SB_HEAD_DIM:(g + 1) * SB_HEAD_DIM] = acc[g * tq:(g + 1) * tq].astype(o_ref.dtype)


def sb_prompt(hmat, batch, seq, tq, tk):
    nq = seq // tq
    gw = SB_GROUP * SB_HEAD_DIM
    k_col0 = SB_HEADS * SB_HEAD_DIM // SB_HEAD_DIM
    v_col0 = k_col0 + SB_KV_HEADS
    return pl.pallas_call(
        functools.partial(_sb_prompt_kernel, tq=tq, tk=tk), grid=(batch, SB_KV_HEADS, nq),
        in_specs=[pl.BlockSpec((tq, gw), lambda b, kv, i: (b * nq + i, kv)),
                  pl.BlockSpec((seq, SB_HEAD_DIM), lambda b, kv, i: (b, k_col0 + kv)),
                  pl.BlockSpec((seq, SB_HEAD_DIM), lambda b, kv, i: (b, v_col0 + kv)),
                  pl.BlockSpec((tk, 2 * tk), lambda b, kv, i: (0, 0))],
        out_specs=pl.BlockSpec((tq, gw), lambda b, kv, i: (b * nq + i, kv)),
        out_shape=jax.ShapeDtypeStruct((batch * seq, SB_HEADS * SB_HEAD_DIM), BF16),
        compiler_params=_params(("parallel", "parallel", "arbitrary")))(hmat, hmat, hmat, _tri_sum_matrix(tk))


def _paged_pipeline(pt_ref, n_pages, ppc, copies_for, compute_chunk, state):
    b = pl.program_id(0)
    nb = pl.num_programs(0)
    nc = n_pages // ppc

    def start_chunk(bb, c, slot):
        for i in range(ppc):
            for k, cp in enumerate(copies_for(pt_ref[bb * n_pages + c * ppc + i], slot, i)):
                cp.start(priority=(i + k) % N_DMA_THREADS)

    def wait_chunk(slot):
        for i in range(ppc):
            for cp in copies_for(0, slot, i):
                cp.wait()

    @pl.when(b == 0)
    def _():
        start_chunk(0, 0, 0)

    def body(c, st):
        slot = (b * nc + c) & 1
        more = c + 1 < nc

        @pl.when(more | (b + 1 < nb))
        def _():
            start_chunk(jnp.where(more, b, b + 1), jnp.where(more, c + 1, 0), 1 - slot)

        wait_chunk(slot)
        return compute_chunk(slot, st)

    return lax.fori_loop(0, nc, body, state)


def _sb_sample_kernel(pt_ref, q_ref, kn_ref, vn_ref, w2_ref, ck_hbm, cv_hbm, o_ref, kbuf, vbuf, sem, *,
                      n_pages, ppc, n_new):
    dh = SB_HEAD_DIM
    rows = n_new * SB_GROUP
    page_rows = SB_KV_HEADS * PAGE_SIZE
    q = (q_ref[0] * SB_SCALE).astype(BF16)
    qs = [q[h] for h in range(SB_KV_HEADS)]
    w2 = w2_ref[...]

    def heads(ref, start, n_keys):
        return [ref[pl.ds(start + h, n_keys, stride=SB_KV_HEADS), :].astype(BF16) for h in range(SB_KV_HEADS)]

    tpos = lax.broadcasted_iota(I32, (rows, PAGE_SIZE), 0) // SB_GROUP
    mask = lax.broadcasted_iota(I32, (rows, PAGE_SIZE), 1) < tpos
    zero = (jnp.zeros((rows, dh), F32), jnp.zeros((rows, PAGE_SIZE), F32))
    st = _sb_chunk(qs, heads(kn_ref.at[0], 0, PAGE_SIZE), heads(vn_ref.at[0], 0, PAGE_SIZE), w2, mask,
                   zero * SB_KV_HEADS)

    def copies_for(page_id, slot, i):
        dst = pl.ds((slot * ppc + i) * page_rows, page_rows)
        return (pltpu.make_async_copy(ck_hbm.at[page_id], kbuf.at[dst], sem.at[0, slot]),
                pltpu.make_async_copy(cv_hbm.at[page_id], vbuf.at[dst], sem.at[1, slot]))

    def compute_chunk(slot, st):
        start = slot * ppc * page_rows
        n_keys = ppc * PAGE_SIZE
        return _sb_chunk(qs, heads(kbuf, start, n_keys), heads(vbuf, start, n_keys), w2, None, st)

    st = _paged_pipeline(pt_ref, n_pages, ppc, copies_for, compute_chunk, st)
    for h in range(SB_KV_HEADS):
        o_ref[0, h] = st[2 * h]


def sb_sample(pt_rev, q, k_new, v_new, cache_k, cache_v, ppc):
    db, kvh, rows, dh = q.shape
    n_pages = pt_rev.shape[0] // db
    page_rows = kvh * PAGE_SIZE
    grid_spec = pltpu.PrefetchScalarGridSpec(
        num_scalar_prefetch=1, grid=(db,),
        in_specs=[pl.BlockSpec((1, kvh, rows, dh), lambda b, pt: (b, 0, 0, 0)),
                  pl.BlockSpec((1, page_rows, dh), lambda b, pt: (b, 0, 0)),
                  pl.BlockSpec((1, page_rows, dh), lambda b, pt: (b, 0, 0)),
                  pl.BlockSpec((PAGE_SIZE, 2 * PAGE_SIZE), lambda b, pt: (0, 0)),
                  pl.BlockSpec(memory_space=pl.ANY), pl.BlockSpec(memory_space=pl.ANY)],
        out_specs=pl.BlockSpec((1, kvh, rows, dh), lambda b, pt: (b, 0, 0, 0)),
        scratch_shapes=[pltpu.VMEM((2 * ppc * page_rows, dh), F32), pltpu.VMEM((2 * ppc * page_rows, dh), F32),
                        pltpu.SemaphoreType.DMA((2, 2))])
    return pl.pallas_call(
        functools.partial(_sb_sample_kernel, n_pages=n_pages, ppc=ppc, n_new=rows // SB_GROUP),
        grid_spec=grid_spec, out_shape=jax.ShapeDtypeStruct(q.shape, F32),
        compiler_params=_params(("arbitrary",)))(pt_rev, q, k_new, v_new, _tri_sum_matrix(PAGE_SIZE), cache_k, cache_v)


def _softmax_update(s, v, m, l, acc):
    m_new = jnp.maximum(m, jnp.max(s, axis=-1, keepdims=True))
    alpha = jnp.exp(m - m_new)
    p = jnp.exp(s - m_new)
    l = l * alpha + jnp.sum(p, axis=-1, keepdims=True)
    acc = acc * alpha + jnp.dot(p.astype(BF16), v, preferred_element_type=F32)
    return m_new, l, acc


def _mla_prompt_kernel(q_ref, k_ref, wuv_ref, o_ref, qs_ref, acc_ref, m_ref, l_ref, *, tq, tk):
    i = pl.program_id(1)
    rows = MLA_HEADS * tq
    for h in range(MLA_HEADS):
        qs_ref[h * tq:(h + 1) * tq, :] = q_ref[:, h * QK_CAT:(h + 1) * QK_CAT]
    acc_ref[...] = jnp.zeros(acc_ref.shape, F32)
    m_ref[...] = jnp.full(m_ref.shape, NEG, F32)
    l_ref[...] = jnp.zeros(l_ref.shape, F32)

    def step(j, masked):
        k = k_ref[pl.ds(pl.multiple_of(j * tk, tk), tk), :]
        s = lax.dot_general(qs_ref[...], k, NT, preferred_element_type=F32)
        if masked:
            qpos = i * tq + lax.broadcasted_iota(I32, (rows, tk), 0) % tq
            s = jnp.where(j * tk + lax.broadcasted_iota(I32, (rows, tk), 1) <= qpos, s, NEG)
        m_prev = m_ref[...]
        m_new = jnp.maximum(m_prev, jnp.max(s, axis=-1, keepdims=True))
        alpha = jnp.exp(m_prev - m_new)
        p = jnp.exp(s - _lanes(m_new, tk))
        l_ref[...] = alpha * l_ref[...] + jnp.sum(p, axis=-1, keepdims=True)
        m_ref[...] = m_new
        acc_ref[...] = acc_ref[...] * _lanes(alpha, KV_LORA) + jnp.dot(p.astype(BF16), k[:, :KV_LORA],
                                                                    preferred_element_type=F32)

    n_full = (i * tq + 1) // tk
    n_all = (i * tq + tq - 1) // tk + 1

    def full_body(j, c):
        step(j, False)
        return c

    def edge_body(j, c):
        step(j, True)
        return c

    lax.fori_loop(0, n_full, full_body, 0)
    lax.fori_loop(n_full, n_all, edge_body, 0)
    for h in range(MLA_HEADS):
        sl = slice(h * tq, (h + 1) * tq)
        lat = (acc_ref[sl, :] / _lanes(l_ref[sl, :], KV_LORA)).astype(BF16)
        o_ref[:, h * V_DIM:(h + 1) * V_DIM] = jnp.dot(lat, wuv_ref[h], preferred_element_type=F32).astype(o_ref.dtype)


def mla_prompt(qcat, kcat, wuv, batch, seq, tq, tk):
    nq = seq // tq
    rows = MLA_HEADS * tq
    return pl.pallas_call(
        functools.partial(_mla_prompt_kernel, tq=tq, tk=tk), grid=(batch, nq),
        in_specs=[pl.BlockSpec((tq, MLA_HEADS * QK_CAT), lambda b, i: (b * nq + i, 0)),
                  pl.BlockSpec((seq, QK_CAT), lambda b, i: (b, 0)),
                  pl.BlockSpec(wuv.shape, lambda b, i: (0, 0, 0))],
        out_specs=pl.BlockSpec((tq, MLA_HEADS * V_DIM), lambda b, i: (b * nq + i, 0)),
        out_shape=jax.ShapeDtypeStruct((batch * seq, MLA_HEADS * V_DIM), BF16),
        scratch_shapes=[pltpu.VMEM((rows, QK_CAT), BF16), pltpu.VMEM((rows, KV_LORA), F32),
                        pltpu.VMEM((rows, LANES), F32), pltpu.VMEM((rows, LANES), F32)],
        compiler_params=_params(("parallel", "arbitrary")))(qcat, kcat, wuv)


def _mla_sample_kernel(pt_ref, q_ref, kn_ref, c_hbm, r_hbm, o_ref, cbuf, rbuf, sem, *, n_pages, ppc, n_new):
    rows = n_new * MLA_HEADS
    q = q_ref[0]
    q_lat, q_rope = q[:, :KV_LORA], q[:, KV_LORA:]
    kn = kn_ref[0]
    tpos = lax.broadcasted_iota(I32, (rows, PAGE_SIZE), 0) // MLA_HEADS
    mask = lax.broadcasted_iota(I32, (rows, PAGE_SIZE), 1) <= tpos
    s_new = jnp.where(mask, lax.dot_general(q, kn, NT, preferred_element_type=F32), NEG)
    init = (jnp.full((rows, 1), NEG, F32), jnp.zeros((rows, 1), F32), jnp.zeros((rows, KV_LORA), F32))
    st = _softmax_update(s_new, kn[:, :KV_LORA], *init)

    def copies_for(page_id, slot, i):
        return (pltpu.make_async_copy(c_hbm.at[page_id], cbuf.at[slot, pl.ds(i * PAGE_SIZE, PAGE_SIZE)], sem.at[0, slot]),
                pltpu.make_async_copy(r_hbm.at[page_id], rbuf.at[slot, i], sem.at[1, slot]))

    n_keys = ppc * PAGE_SIZE
    rope_pad = jnp.zeros((LANES - ROPE, n_keys), BF16)

    def compute_chunk(slot, st):
        c = cbuf[slot].astype(BF16)
        krt = jnp.concatenate([rbuf[slot, i] for i in range(ppc)], axis=1).astype(BF16)
        krt = jnp.concatenate([krt, rope_pad], axis=0)
        s = (lax.dot_general(q_lat, c, NT, preferred_element_type=F32)
             + jnp.dot(q_rope, krt, preferred_element_type=F32))
        return _softmax_update(s, c, *st)

    _, l, acc = _paged_pipeline(pt_ref, n_pages, ppc, copies_for, compute_chunk, st)
    o_ref[0] = (acc / l).astype(o_ref.dtype)


def mla_sample(pt, q, k_new, cache_ckv, cache_krt, ppc):
    db, rows, _ = q.shape
    n_pages = pt.shape[0] // db
    grid_spec = pltpu.PrefetchScalarGridSpec(
        num_scalar_prefetch=1, grid=(db,),
        in_specs=[pl.BlockSpec((1, rows, QK_CAT), lambda b, pt: (b, 0, 0)),
                  pl.BlockSpec((1, PAGE_SIZE, QK_CAT), lambda b, pt: (b, 0, 0)),
                  pl.BlockSpec(memory_space=pl.ANY), pl.BlockSpec(memory_space=pl.ANY)],
        out_specs=pl.BlockSpec((1, rows, KV_LORA), lambda b, pt: (b, 0, 0)),
        scratch_shapes=[pltpu.VMEM((2, ppc * PAGE_SIZE, KV_LORA), F32), pltpu.VMEM((2, ppc, ROPE, PAGE_SIZE), F32),
                        pltpu.SemaphoreType.DMA((2, 2))])
    return pl.pallas_call(
        functools.partial(_mla_sample_kernel, n_pages=n_pages, ppc=ppc, n_new=rows // MLA_HEADS),
        grid_spec=grid_spec, out_shape=jax.ShapeDtypeStruct((db, rows, KV_LORA), BF16),
        compiler_params=_params(("arbitrary",)))(pt, q, k_new, cache_ckv, cache_krt)


def _head_mm_kernel(a_ref, w_ref, o_ref):
    o_ref[...] = jnp.dot(a_ref[...], w_ref[0], preferred_element_type=F32).astype(o_ref.dtype)


def head_matmul(a, w, out_dtype):
    m = a.shape[0]
    h, k, n = w.shape
    return pl.pallas_call(
        _head_mm_kernel, grid=(h,),
        in_specs=[pl.BlockSpec((m, k), lambda i: (0, i)), pl.BlockSpec((1, k, n), lambda i: (i, 0, 0))],
        out_specs=pl.BlockSpec((m, n), lambda i: (0, i)),
        out_shape=jax.ShapeDtypeStruct((m, h * n), out_dtype),
        compiler_params=_params(("parallel",)))(a, w)


MEM_PAIR = 2


def _mem_attend(q, k, v):
    s = lax.dot_general((q * MEM_SCALE).astype(BF16), k.astype(BF16), NT, preferred_element_type=F32)
    p = jnp.exp(s - jnp.max(s, axis=-1, keepdims=True))
    o = jnp.dot(p.astype(BF16), v.astype(BF16), preferred_element_type=F32)
    return o / jnp.sum(p, axis=-1, keepdims=True)


def _mem_prompt_kernel(q_ref, k_ref, v_ref, o_ref):
    for h in range(MEM_PAIR):
        sl = slice(h * MEM_DIM, (h + 1) * MEM_DIM)
        o_ref[:, sl] = _mem_attend(q_ref[:, sl], k_ref[:, sl], v_ref[:, sl]).astype(o_ref.dtype)


def mem_attn_prompt(hmat, memkv, q_col0, batch, seq, n_mem, tq):
    nq = seq // tq
    w = MEM_PAIR * MEM_DIM
    n_pair = MEM_HEADS // MEM_PAIR
    return pl.pallas_call(
        _mem_prompt_kernel, grid=(batch, n_pair, nq),
        in_specs=[pl.BlockSpec((tq, w), lambda b, hp, i: (b * nq + i, q_col0 + hp)),
                  pl.BlockSpec((n_mem, w), lambda b, hp, i: (b, hp)),
                  pl.BlockSpec((n_mem, w), lambda b, hp, i: (b, n_pair + hp))],
        out_specs=pl.BlockSpec((tq, w), lambda b, hp, i: (b * nq + i, hp)),
        out_shape=jax.ShapeDtypeStruct((batch * seq, MEM_HEADS * MEM_DIM), BF16),
        compiler_params=_params(("parallel", "parallel", "arbitrary")))(hmat, memkv, memkv)


def _mem_sample_kernel(qa_ref, qb_ref, k_ref, v_ref, o_ref, *, n_new, group, n_mem):
    parts = MEM_DIM // LANES
    stride = MEM_HEADS * parts
    q = jnp.concatenate([qa_ref[...], qb_ref[...]], axis=-1)
    row = lax.broadcasted_iota(I32, (q.shape[0], MEM_DIM), 0) // n_new

    def head(ref, g, h):
        return jnp.concatenate([ref[g, pl.ds(p * MEM_HEADS + h, n_mem, stride=stride), :] for p in range(parts)], axis=-1)

    for h in range(MEM_HEADS):
        qh = q[:, h * MEM_DIM:(h + 1) * MEM_DIM]
        out = jnp.zeros(qh.shape, F32)
        for g in range(group):
            out = jnp.where(row == g, _mem_attend(qh, head(k_ref, g, h), head(v_ref, g, h)), out)
        o_ref[:, h * MEM_DIM:(h + 1) * MEM_DIM] = out.astype(o_ref.dtype)


def mem_sample(hmat, cache_k, cache_v, q_col0, row0, n_new, n_mem):
    db, cache_rows, _ = cache_k.shape
    group = 8 // n_new
    rows = group * n_new
    w = MEM_PAIR * MEM_DIM
    r0 = row0 // rows
    return pl.pallas_call(
        functools.partial(_mem_sample_kernel, n_new=n_new, group=group, n_mem=n_mem), grid=(db // group,),
        in_specs=[pl.BlockSpec((rows, w), lambda i: (r0 + i, q_col0)),
                  pl.BlockSpec((rows, w), lambda i: (r0 + i, q_col0 + 1)),
                  pl.BlockSpec((group, cache_rows, LANES), lambda i: (i, 0, 0)),
                  pl.BlockSpec((group, cache_rows, LANES), lambda i: (i, 0, 0))],
        out_specs=pl.BlockSpec((rows, MEM_HEADS * MEM_DIM), lambda i: (i, 0)),
        out_shape=jax.ShapeDtypeStruct((db * n_new, MEM_HEADS * MEM_DIM), BF16),
        compiler_params=_params(("arbitrary",)))(hmat, hmat, cache_k, cache_v)


def _mix_kernel(osb_ref, omla_ref, omem_ref, wsb_ref, wmla_ref, wmem_ref, gsb_ref, gmla_ref, gmem_ref, o_ref):
    mix = jnp.zeros(o_ref.shape, F32)
    for o, w, g in ((osb_ref, wsb_ref, gsb_ref), (omla_ref, wmla_ref, gmla_ref), (omem_ref, wmem_ref, gmem_ref)):
        mix = mix + jax.nn.sigmoid(g[...]) * jnp.dot(o[...], w[...], preferred_element_type=F32)
    o_ref[...] = mix.astype(o_ref.dtype)


def branch_mix(o_sb, o_mla, o_mem, w_sb, w_mla, w_mem, hmat, gate_col0, d, tm, tn):
    t, kdim = o_sb.shape
    nn = d // tn
    a_spec = pl.BlockSpec((tm, kdim), lambda i, j: (i, 0))
    w_spec = pl.BlockSpec((kdim, tn), lambda i, j: (0, j))
    g_specs = [pl.BlockSpec((tm, tn), lambda i, j, br=br: (i, gate_col0 + br * nn + j)) for br in range(3)]
    return pl.pallas_call(
        _mix_kernel, grid=(t // tm, nn),
        in_specs=[a_spec, a_spec, a_spec, w_spec, w_spec, w_spec] + g_specs,
        out_specs=pl.BlockSpec((tm, tn), lambda i, j: (i, j)),
        out_shape=jax.ShapeDtypeStruct((t, d), BF16),
        compiler_params=_params(("parallel", "arbitrary")))(o_sb, o_mla, o_mem, w_sb, w_mla, w_mem, hmat, hmat, hmat)


def _out_proj_kernel(mix_ref, w_ref, x_ref, g_ref, rhi_ref, rlo_ref, b_ref, xmid_ref, hn_ref, lg_ref):
    xmid = x_ref[...] + jnp.dot(mix_ref[...], w_ref[...], preferred_element_type=F32)
    xmid_ref[...] = xmid
    hn = _rms(xmid, g_ref[...])
    tm, d = hn.shape
    spr = d // LANES
    for c in range(spr):
        hn_ref[pl.ds(c, tm, stride=spr), :] = hn[:, c * LANES:(c + 1) * LANES]
    hi = hn.astype(BF16)
    lo = (hn - hi.astype(F32)).astype(BF16)
    lg = (jnp.dot(hi, rhi_ref[...], preferred_element_type=F32) + jnp.dot(lo, rhi_ref[...], preferred_element_type=F32)
          + jnp.dot(hi, rlo_ref[...], preferred_element_type=F32))
    lg_ref[...] = lg + b_ref[...]


def out_proj(mix, w_out, x, g_ffn, r_hi, r_lo, r_bias, tm):
    t, d = x.shape
    row = lambda i: (i, 0)
    fix = lambda i: (0, 0)
    return pl.pallas_call(
        _out_proj_kernel, grid=(t // tm,),
        in_specs=[pl.BlockSpec((tm, d), row), pl.BlockSpec((d, d), fix), pl.BlockSpec((tm, d), row),
                  pl.BlockSpec((1, d), fix), pl.BlockSpec((d, LANES), fix), pl.BlockSpec((d, LANES), fix),
                  pl.BlockSpec((1, LANES), fix)],
        out_specs=[pl.BlockSpec((tm, d), row), pl.BlockSpec((tm * (d // LANES), LANES), row),
                   pl.BlockSpec((tm, LANES), row)],
        out_shape=[jax.ShapeDtypeStruct((t, d), F32), jax.ShapeDtypeStruct((t * (d // LANES), LANES), F32),
                   jax.ShapeDtypeStruct((t, LANES), F32)],
        compiler_params=_params(("parallel",)))(mix, w_out, x, g_ffn.reshape(1, d), r_hi, r_lo, r_bias)


def _route_kernel(lg_ref, o_ref):
    lg = lg_ref[...]
    lane = lax.broadcasted_iota(I32, lg.shape, 1)
    lanef = lane.astype(F32)

    def first_max(x):
        v = jnp.max(x, axis=-1, keepdims=True)
        return v, jnp.min(jnp.where(x == v, lanef, 1e6), axis=-1, keepdims=True)

    is_grp = lane < N_GROUPS
    gl = jnp.where(is_grp, lg, NEG)
    gmax, g_idx = first_max(gl)
    g_p = 1.0 / jnp.sum(jnp.where(is_grp, jnp.exp(gl - gmax), 0.0), axis=-1, keepdims=True)
    grp_of_lane = ((lane - N_GROUPS) >> 3).astype(F32)
    el = jnp.where((lane >= N_GROUPS) & (grp_of_lane == g_idx), lg, NEG)
    v1, i1 = first_max(el)
    v2, i2 = first_max(jnp.where(lanef == i1, NEG, el))
    d = jnp.exp(v2 - v1)
    w1 = g_p / (1.0 + d)
    w2 = g_p * d / (1.0 + d)
    e1 = i1 - N_GROUPS
    e2 = i2 - N_GROUPS
    o_ref[...] = jnp.where(lane == 0, e1, jnp.where(lane == 1, e2, jnp.where(lane == 2, w1, jnp.where(lane == 3, w2, 0.0))))


def route(logits, tm):
    t = logits.shape[0]
    return pl.pallas_call(
        _route_kernel, grid=(t // tm,),
        in_specs=[pl.BlockSpec((tm, LANES), lambda i: (i, 0))], out_specs=pl.BlockSpec((tm, LANES), lambda i: (i, 0)),
        out_shape=jax.ShapeDtypeStruct((t, LANES), F32), compiler_params=_params(("parallel",)))(logits)


def _slab_copies(idx_ref, base, n, src_hbm, dst, dst_row0, spr, sem):
    cps = []
    for r in range(n):
        src_row = pl.multiple_of(idx_ref[base + r] * spr, spr)
        cps.append(pltpu.make_async_copy(src_hbm.at[pl.ds(src_row, spr)], dst.at[pl.ds(dst_row0 + r * spr, spr)], sem))
    return cps


def _expert_kernel(tok_ref, be_ref, first_ref, wslot_ref, nexte_ref, nused_ref, h_hbm, wg_hbm, wu_hbm, wd_hbm, o_ref,
                   xbuf, wg_f32, wu_f32, wd_f32, wg_bf, wu_bf, wd_bf, xsem, wsem, *, spr):
    u = pl.program_id(0)
    n_used = nused_ref[0]
    blk_rows = MOE_BLOCK * spr

    def gather(blk, slot):
        return _slab_copies(tok_ref, blk * MOE_BLOCK, MOE_BLOCK, h_hbm, xbuf, slot * blk_rows, spr, xsem.at[slot])

    def weights(e, slot):
        return (pltpu.make_async_copy(wg_hbm.at[e], wg_f32.at[slot], wsem.at[slot]),
                pltpu.make_async_copy(wu_hbm.at[e], wu_f32.at[slot], wsem.at[slot]),
                pltpu.make_async_copy(wd_hbm.at[e], wd_f32.at[slot], wsem.at[slot]))

    @pl.when(u == 0)
    def _():
        for cp in weights(be_ref[0], 0):
            cp.start(priority=1)
        for cp in gather(0, 0):
            cp.start()

    @pl.when(u + 1 < n_used)
    def _():
        for cp in gather(u + 1, (u + 1) & 1):
            cp.start()

    @pl.when(u < n_used)
    def _():
        @pl.when(first_ref[u] == 1)
        def _():
            ws = wslot_ref[u]
            nxt = nexte_ref[u]

            @pl.when(nxt >= 0)
            def _():
                for cp in weights(nxt, 1 - ws):
                    cp.start(priority=1)

            for cp in weights(0, ws):
                cp.wait()
            wg_bf[...] = wg_f32[ws].astype(BF16)
            wu_bf[...] = wu_f32[ws].astype(BF16)
            wd_bf[...] = wd_f32[ws].astype(BF16)

        slot = u & 1
        for cp in gather(u, slot):
            cp.wait()
        x = jnp.concatenate([xbuf[pl.ds(slot * blk_rows + c, MOE_BLOCK, stride=spr), :] for c in range(spr)], axis=-1)
        x = x.astype(BF16)
        gate = jnp.dot(x, wg_bf[...], preferred_element_type=F32)
        up = jnp.dot(x, wu_bf[...], preferred_element_type=F32)
        a = (gate * jax.nn.sigmoid(gate) * up).astype(BF16)
        y = jnp.dot(a, wd_bf[...], preferred_element_type=F32)
        for c in range(spr):
            o_ref[pl.ds(c, MOE_BLOCK, stride=spr), :] = y[:, c * LANES:(c + 1) * LANES]

    @pl.when(u >= n_used)
    def _():
        o_ref[...] = jnp.zeros(o_ref.shape, o_ref.dtype)


def moe_experts(slot_tok, plan, h_slabs, w_gate, w_up, w_down):
    n_slots = slot_tok.shape[0]
    _, d, de = w_gate.shape
    spr = d // LANES
    grid_spec = pltpu.PrefetchScalarGridSpec(
        num_scalar_prefetch=6, grid=(n_slots // MOE_BLOCK,),
        in_specs=[pl.BlockSpec(memory_space=pl.ANY)] * 4,
        out_specs=pl.BlockSpec((MOE_BLOCK * spr, LANES), lambda u, *_: (u, 0)),
        scratch_shapes=[pltpu.VMEM((2 * MOE_BLOCK * spr, LANES), F32),
                        pltpu.VMEM((2, d, de), F32), pltpu.VMEM((2, d, de), F32), pltpu.VMEM((2, de, d), F32),
                        pltpu.VMEM((d, de), BF16), pltpu.VMEM((d, de), BF16), pltpu.VMEM((de, d), BF16),
                        pltpu.SemaphoreType.DMA((2,)), pltpu.SemaphoreType.DMA((2,))])
    return pl.pallas_call(functools.partial(_expert_kernel, spr=spr), grid_spec=grid_spec,
                          out_shape=jax.ShapeDtypeStruct((n_slots * spr, LANES), F32),
                          compiler_params=_params(("arbitrary",)))(slot_tok, *plan, h_slabs, w_gate, w_up, w_down)


def _combine_kernel(s1_ref, s2_ref, x_ref, rt_ref, g_ref, y_hbm, o_ref, ybuf, sem, *, tm, spr):
    i = pl.program_id(0)
    n = pl.num_programs(0)
    half_rows = tm * spr

    def gather(step, slot):
        base = step * tm
        return (_slab_copies(s1_ref, base, tm, y_hbm, ybuf, (2 * slot) * half_rows, spr, sem.at[slot])
                + _slab_copies(s2_ref, base, tm, y_hbm, ybuf, (2 * slot + 1) * half_rows, spr, sem.at[slot]))

    @pl.when(i == 0)
    def _():
        for k, cp in enumerate(gather(0, 0)):
            cp.start(priority=k % N_DMA_THREADS)

    @pl.when(i + 1 < n)
    def _():
        for k, cp in enumerate(gather(i + 1, (i + 1) & 1)):
            cp.start(priority=k % N_DMA_THREADS)

    slot = i & 1
    for cp in gather(i, slot):
        cp.wait()
    rt = rt_ref[...]
    w1, w2 = rt[:, 2:3], rt[:, 3:4]
    ss = jnp.zeros((tm, 1), F32)
    for c in range(spr):
        sl = slice(c * LANES, (c + 1) * LANES)
        y1 = ybuf[pl.ds((2 * slot) * half_rows + c, tm, stride=spr), :]
        y2 = ybuf[pl.ds((2 * slot + 1) * half_rows + c, tm, stride=spr), :]
        xc = x_ref[:, sl] + (w1 * y1 + w2 * y2)
        ss = ss + jnp.sum(xc * xc, axis=-1, keepdims=True)
        o_ref[:, sl] = xc
    scale = lax.rsqrt(ss / (spr * LANES) + NORM_EPS)
    o_ref[...] = o_ref[...] * scale * g_ref[...]


def moe_combine(slot1, slot2, xmid, routed, g_final, y_slabs, tm):
    t, d = xmid.shape
    spr = d // LANES
    grid_spec = pltpu.PrefetchScalarGridSpec(
        num_scalar_prefetch=2, grid=(t // tm,),
        in_specs=[pl.BlockSpec((tm, d), lambda i, a, b: (i, 0)), pl.BlockSpec((tm, LANES), lambda i, a, b: (i, 0)),
                  pl.BlockSpec((1, d), lambda i, a, b: (0, 0)), pl.BlockSpec(memory_space=pl.ANY)],
        out_specs=pl.BlockSpec((tm, d), lambda i, a, b: (i, 0)),
        scratch_shapes=[pltpu.VMEM((4 * tm * spr, LANES), F32), pltpu.SemaphoreType.DMA((2,))])
    return pl.pallas_call(functools.partial(_combine_kernel, tm=tm, spr=spr), grid_spec=grid_spec,
                          out_shape=jax.ShapeDtypeStruct((t, d), F32),
                          compiler_params=_params(("arbitrary",)))(slot1, slot2, xmid, routed, g_final.reshape(1, d), y_slabs)


def _dispatch_plan(e12, n_exp):
    t = e12.shape[0]
    flat_e = e12.reshape(-1)
    tk = flat_e.shape[0]
    n_blocks = -(-tk // MOE_BLOCK) + n_exp
    order = jnp.argsort(flat_e).astype(I32)
    se = flat_e[order]
    counts = jnp.bincount(flat_e, length=n_exp).astype(I32)
    start = jnp.cumsum(counts) - counts
    padded = (counts + MOE_BLOCK - 1) // MOE_BLOCK * MOE_BLOCK
    pad_end = jnp.cumsum(padded)
    dest = (pad_end - padded)[se] + jnp.arange(tk, dtype=I32) - start[se]
    slot_of = jnp.zeros((tk,), I32).at[order].set(dest).reshape(t, 2)
    slot_tok = jnp.zeros((n_blocks * MOE_BLOCK,), I32).at[dest].set(order // 2)
    blocks = jnp.arange(n_blocks, dtype=I32)
    block_e = jnp.minimum(jnp.searchsorted(pad_end, blocks * MOE_BLOCK, side='right'), n_exp - 1).astype(I32)
    n_used = (pad_end[-1:] // MOE_BLOCK).astype(I32)
    ids = jnp.arange(n_exp, dtype=I32)
    busy = counts > 0
    nxt = lax.cummin(jnp.where(busy, ids, n_exp)[::-1])[::-1]
    next_busy = jnp.concatenate([nxt[1:], jnp.full((1,), n_exp, I32)])
    run = jnp.cumsum(busy.astype(I32)) - 1
    first = ((blocks == ((pad_end - padded) // MOE_BLOCK)[block_e]) & (blocks < n_used[0])).astype(I32)
    wslot = run[block_e] & 1
    next_e = jnp.where(next_busy[block_e] < n_exp, next_busy[block_e], -1).astype(I32)
    return slot_of, slot_tok, (block_e, first, wslot, next_e, n_used)


def _tile(n, target, mult):
    best = None
    for c in range(mult, min(n, target) + 1, mult):
        if n % c == 0:
            best = c
    assert best is not None, (n, target, mult)
    return best


def kernel(x_prompt, x_sample, mem_prompt, cache_sb_k, cache_sb_v, cache_mla_ckv, cache_mla_krope, cache_mem_k, cache_mem_v, page_table, g_attn_norm, w_in, g_q_norm, w_uq, g_kv_norm, w_uk, w_uv, g_mem_norm, w_mem_kv, w_br_sb, w_br_mla, w_br_mem, w_out, g_ffn_norm, w_grp, b_grp, w_rtr, b_rtr, w_gate, w_up, w_down, g_final):
    depth = g_attn_norm.shape[0]
    assert depth == 1
    batch, seq, d = x_prompt.shape
    db, n_new, _ = x_sample.shape
    n_mem = mem_prompt.shape[1]
    n_pages = page_table.shape[1]
    n_pool = cache_sb_k.shape[1]
    tp, ts = batch * seq, db * n_new
    t = tp + ts
    l = 0

    sbq_w, sbkv_w = SB_HEADS * SB_HEAD_DIM, SB_KV_HEADS * SB_HEAD_DIM
    memq_w = MEM_HEADS * MEM_DIM
    c_kr = 2 * sbkv_w + sbq_w + Q_LORA + KV_LORA
    w = w_in[l]
    wp = jnp.concatenate([w[:, :c_kr], w[:, c_kr + ROPE:], w[:, c_kr:c_kr + ROPE], jnp.zeros((d, LANES - ROPE), F32)],
                         axis=1).astype(BF16)
    col_cq = sbq_w + 2 * sbkv_w
    col_ckv = col_cq + Q_LORA
    col_memq = col_ckv + KV_LORA
    col_gate = col_memq + memq_w
    col_kr = col_gate + 3 * d
    n_cols = wp.shape[1]

    half = ROPE // 2
    inv = ROPE_THETA ** (-jnp.arange(half, dtype=F32) / half)
    pos = jnp.concatenate([jnp.tile(jnp.arange(seq), batch), jnp.tile(n_pages * PAGE_SIZE + jnp.arange(n_new), db)])
    ang = pos.astype(F32)[:, None] * inv[None, :]
    cos, sin = jnp.cos(ang), jnp.sin(ang)
    zeros = jnp.zeros_like(cos)
    cos_q, sin_q = jnp.tile(cos, (1, MLA_HEADS)), jnp.tile(sin, (1, MLA_HEADS))
    c_tab = jnp.concatenate([cos, cos, zeros, zeros], axis=1)
    s1_tab = jnp.concatenate([-sin, zeros, zeros, zeros], axis=1)
    s2_tab = jnp.concatenate([zeros, sin, zeros, zeros], axis=1)

    wuq = w_uq[l]
    wuq2 = jnp.concatenate([wuq[:, :, :NOPE].reshape(Q_LORA, -1), wuq[:, :, NOPE:NOPE + half].reshape(Q_LORA, -1),
                            wuq[:, :, NOPE + half:].reshape(Q_LORA, -1)], axis=1).astype(BF16)
    wuk_t = jnp.transpose(w_uk[l], (1, 2, 0)).astype(BF16)
    wuv = jnp.transpose(w_uv[l], (1, 0, 2)).astype(BF16)
    hh = jnp.arange(MLA_HEADS)[:, None]
    ii = jnp.arange(half)[None, :]
    sel = jnp.zeros((2 * MLA_HEADS * half, MLA_HEADS * LANES), BF16)
    sel = sel.at[(hh * half + ii).ravel(), (hh * LANES + ii).ravel()].set(1.0)
    sel = sel.at[(MLA_HEADS * half + hh * half + ii).ravel(), (hh * LANES + half + ii).ravel()].set(1.0)

    xcat = jnp.concatenate([x_prompt.reshape(tp, d), x_sample.reshape(ts, d)], axis=0)
    tm_big = _tile(t, 1152, 16)
    hn_attn = rmsnorm(xcat, g_attn_norm[l], BF16, _tile(t, 640, 16))
    hmat = matmul(hn_attn, wp, F32, tm_big, _tile(n_cols, 1024, LANES))

    tm_mid = _tile(t, 640, 16)
    qcat = mla_q(hmat, col_cq // Q_LORA, g_q_norm[l], wuq2, wuk_t, sel, cos_q, sin_q, tm_mid)
    ckv_n, kr_rot, kcat = mla_kv(hmat, col_ckv // KV_LORA, col_kr // LANES, g_kv_norm[l], c_tab, s1_tab, s2_tab, tm_mid)

    memn = rmsnorm(mem_prompt.reshape(batch * n_mem, d), g_mem_norm[l], BF16, _tile(batch * n_mem, 512, 16))
    memkv = matmul(memn, w_mem_kv[l].astype(BF16), F32, _tile(batch * n_mem, 1024, 16), 1024)

    o_sb_p = sb_prompt(hmat, batch, seq, 128, 256)
    o_mla_p = mla_prompt(qcat, kcat, wuv, batch, seq, 128, 512)
    o_mem_p = mem_attn_prompt(hmat, memkv, col_memq // (MEM_PAIR * MEM_DIM), batch, seq, n_mem, _tile(seq, 512, 8))

    hs = hmat[tp:]
    rows_sb = n_new * SB_GROUP
    q_sb = hs[:, :sbq_w].reshape(db, n_new, SB_KV_HEADS, SB_GROUP, SB_HEAD_DIM)
    q_sb = jnp.transpose(q_sb, (0, 2, 1, 3, 4)).reshape(db, SB_KV_HEADS, rows_sb, SB_HEAD_DIM)
    page_rows = PAGE_SIZE * SB_KV_HEADS

    def new_kv(cols):
        x = hs[:, cols:cols + sbkv_w].reshape(db, n_new * SB_KV_HEADS, SB_HEAD_DIM)
        return jnp.pad(x, ((0, 0), (0, page_rows - n_new * SB_KV_HEADS), (0, 0)))

    pt = page_table.astype(I32)
    ppc = _tile(n_pages, 16, 1)
    o_sb_s = sb_sample(pt[:, ::-1].reshape(-1), q_sb, new_kv(sbq_w), new_kv(sbq_w + sbkv_w),
                       cache_sb_k[l].reshape(n_pool, page_rows, SB_HEAD_DIM),
                       cache_sb_v[l].reshape(n_pool, page_rows, SB_HEAD_DIM), ppc)
    o_sb_s = jnp.transpose(o_sb_s.reshape(db, SB_KV_HEADS, n_new, SB_GROUP, SB_HEAD_DIM), (0, 2, 1, 3, 4))
    o_sb_s = o_sb_s.reshape(ts, sbq_w).astype(BF16)

    q_mla_s = qcat[tp:].reshape(db, n_new * MLA_HEADS, QK_CAT)
    kcat_new = jnp.pad(kcat[tp:].reshape(db, n_new, QK_CAT), ((0, 0), (0, PAGE_SIZE - n_new), (0, 0)))
    lat_s = mla_sample(pt.reshape(-1), q_mla_s, kcat_new, cache_mla_ckv[l],
                       jnp.transpose(cache_mla_krope[l], (0, 2, 1)), ppc)
    o_mla_s = head_matmul(lat_s.reshape(ts, MLA_HEADS * KV_LORA), wuv, BF16)

    def mem_rows(c):
        c = c.reshape(db, n_mem, MEM_HEADS, MEM_DIM // LANES, LANES)
        return jnp.transpose(c, (0, 1, 3, 2, 4)).reshape(db, n_mem * memq_w // LANES, LANES)

    o_mem_s = mem_sample(hmat, mem_rows(cache_mem_k[l]), mem_rows(cache_mem_v[l]),
                         col_memq // (MEM_PAIR * MEM_DIM), tp, n_new, n_mem)

    o_sb = jnp.concatenate([o_sb_p, o_sb_s], axis=0)
    o_mla = jnp.concatenate([o_mla_p, o_mla_s], axis=0)
    o_mem = jnp.concatenate([o_mem_p, o_mem_s], axis=0)
    tn_mix = 512
    mix = branch_mix(o_sb, o_mla, o_mem, w_br_sb[l].astype(BF16), w_br_mla[l].astype(BF16), w_br_mem[l].astype(BF16),
                     hmat, col_gate // tn_mix, d, tm_big, tn_mix)
    w_r = jnp.concatenate([w_grp[l], w_rtr[l], jnp.zeros((d, LANES - N_GROUPS - N_EXPERTS), F32)], axis=1)
    r_hi = w_r.astype(BF16)
    r_lo = (w_r - r_hi.astype(F32)).astype(BF16)
    r_bias = jnp.concatenate([b_grp[l], b_rtr[l], jnp.zeros((LANES - N_GROUPS - N_EXPERTS,), F32)]).reshape(1, LANES)
    xmid, hn_ffn, logits = out_proj(mix, w_out[l].astype(BF16), xcat, g_ffn_norm[l], r_hi, r_lo, r_bias, _tile(t, 320, 16))

    routed = route(logits, _tile(t, 1152, 8))
    slot_of, slot_tok, plan = _dispatch_plan(routed[:, :2].astype(I32), N_EXPERTS)
    y_slabs = moe_experts(slot_tok, plan, hn_ffn, w_gate[l], w_up[l], w_down[l])
    y = moe_combine(slot_of[:, 0], slot_of[:, 1], xmid, routed, g_final, y_slabs, _tile(t, 128, 8))

    def kv_out(cols, rows, shape):
        return rows[:, cols[0]:cols[1]].reshape(shape)[None]

    hp = hmat[:tp]
    return (y[:tp].reshape(batch, seq, d), y[tp:].reshape(db, n_new, d),
            kv_out((sbq_w, sbq_w + sbkv_w), hp, (batch, seq, SB_KV_HEADS, SB_HEAD_DIM)),
            kv_out((sbq_w + sbkv_w, sbq_w + 2 * sbkv_w), hp, (batch, seq, SB_KV_HEADS, SB_HEAD_DIM)),
            ckv_n[:tp].reshape(batch, seq, KV_LORA)[None], kr_rot[:tp].reshape(batch, seq, ROPE)[None],
            memkv[:, :memq_w].reshape(batch, n_mem, MEM_HEADS, MEM_DIM)[None],
            memkv[:, memq_w:].reshape(batch, n_mem, MEM_HEADS, MEM_DIM)[None],
            kv_out((sbq_w, sbq_w + sbkv_w), hs, (db, n_new, SB_KV_HEADS, SB_HEAD_DIM)),
            kv_out((sbq_w + sbkv_w, sbq_w + 2 * sbkv_w), hs, (db, n_new, SB_KV_HEADS, SB_HEAD_DIM)),
            ckv_n[tp:].reshape(db, n_new, KV_LORA)[None], kr_rot[tp:].reshape(db, n_new, ROPE)[None])
```

```python
import functools

import jax
import jax.numpy as jnp
from jax import lax
from jax.experimental import pallas as pl
from jax.experimental.pallas import tpu as pltpu

F32 = jnp.float32
BF16 = jnp.bfloat16
I32 = jnp.int32

NORM_EPS = 1e-6
ROPE_THETA = 10000.0
PAGE_SIZE = 128
LANES = 128
N_DMA_THREADS = 2
RING = 4
SB_HEADS, SB_KV_HEADS, SB_HEAD_DIM = 8, 2, 128
SB_GROUP = SB_HEADS // SB_KV_HEADS
SB_SCALE = SB_HEAD_DIM ** -0.5
MLA_HEADS, Q_LORA, KV_LORA, NOPE, ROPE, V_DIM = 8, 512, 512, 128, 64, 128
MLA_SCALE = (NOPE + ROPE) ** -0.5
QK_CAT = KV_LORA + LANES
MEM_HEADS, MEM_DIM = 4, 256
MEM_SCALE = MEM_DIM ** -0.5
N_GROUPS, EPG, N_EXPERTS, D_EXPERT = 8, 8, 64, 512
MOE_BLOCK = 128
NEG = -1e30
VMEM_LIMIT = 56 * 1024 * 1024
NT = (((1,), (1,)), ((), ()))


def _params(sem):
    return pltpu.CompilerParams(dimension_semantics=sem, vmem_limit_bytes=VMEM_LIMIT)


def _rms(x, g):
    return x * lax.rsqrt(jnp.mean(x * x, axis=-1, keepdims=True) + NORM_EPS) * g


def _lanes(x, n):
    return x if n == LANES else jnp.concatenate([x] * (n // LANES), axis=1)


def _rmsnorm_kernel(x_ref, g_ref, o_ref):
    o_ref[...] = _rms(x_ref[...], g_ref[...]).astype(o_ref.dtype)


def rmsnorm(x, g, out_dtype, tm):
    r, d = x.shape
    return pl.pallas_call(
        _rmsnorm_kernel, grid=(r // tm,),
        in_specs=[pl.BlockSpec((tm, d), lambda i: (i, 0)), pl.BlockSpec((1, d), lambda i: (0, 0))],
        out_specs=pl.BlockSpec((tm, d), lambda i: (i, 0)),
        out_shape=jax.ShapeDtypeStruct((r, d), out_dtype),
        compiler_params=_params(("parallel",)))(x, g.reshape(1, d))


def _mm_kernel(a_ref, b_ref, o_ref):
    o_ref[...] = jnp.dot(a_ref[...], b_ref[...], preferred_element_type=F32).astype(o_ref.dtype)


def matmul(a, b, out_dtype, tm, tn):
    m, k = a.shape
    n = b.shape[1]
    return pl.pallas_call(
        _mm_kernel, grid=(m // tm, n // tn),
        in_specs=[pl.BlockSpec((tm, k), lambda i, j: (i, 0)), pl.BlockSpec((k, tn), lambda i, j: (0, j))],
        out_specs=pl.BlockSpec((tm, tn), lambda i, j: (i, j)),
        out_shape=jax.ShapeDtypeStruct((m, n), out_dtype),
        compiler_params=_params(("parallel", "arbitrary")))(a, b)


def _mla_q_kernel(cq_ref, g_ref, wuq_ref, wuk_ref, sel_ref, cos_ref, sin_ref, o_ref):
    cqn = _rms(cq_ref[...], g_ref[...]).astype(BF16)
    q = jnp.dot(cqn, wuq_ref[...], preferred_element_type=F32) * MLA_SCALE
    n_np = MLA_HEADS * NOPE
    half = MLA_HEADS * (ROPE // 2)
    r1 = q[:, n_np:n_np + half]
    r2 = q[:, n_np + half:n_np + 2 * half]
    cos, sin = cos_ref[...], sin_ref[...]
    rot = jnp.concatenate([r1 * cos - r2 * sin, r1 * sin + r2 * cos], axis=-1).astype(BF16)
    tail = jnp.dot(rot, sel_ref[...], preferred_element_type=F32)
    for h in range(MLA_HEADS):
        nope = q[:, h * NOPE:(h + 1) * NOPE].astype(BF16)
        lat = jnp.dot(nope, wuk_ref[h], preferred_element_type=F32)
        o_ref[:, h * QK_CAT:h * QK_CAT + KV_LORA] = lat.astype(o_ref.dtype)
        o_ref[:, h * QK_CAT + KV_LORA:(h + 1) * QK_CAT] = tail[:, h * LANES:(h + 1) * LANES].astype(o_ref.dtype)


def mla_q(hmat, cq_col, g_q, wuq, wuk_t, sel, cos_q, sin_q, tm):
    t = hmat.shape[0]
    return pl.pallas_call(
        _mla_q_kernel, grid=(t // tm,),
        in_specs=[pl.BlockSpec((tm, Q_LORA), lambda i: (i, cq_col)),
                  pl.BlockSpec((1, Q_LORA), lambda i: (0, 0)),
                  pl.BlockSpec(wuq.shape, lambda i: (0, 0)),
                  pl.BlockSpec(wuk_t.shape, lambda i: (0, 0, 0)),
                  pl.BlockSpec(sel.shape, lambda i: (0, 0)),
                  pl.BlockSpec((tm, cos_q.shape[1]), lambda i: (i, 0)),
                  pl.BlockSpec((tm, sin_q.shape[1]), lambda i: (i, 0))],
        out_specs=pl.BlockSpec((tm, MLA_HEADS * QK_CAT), lambda i: (i, 0)),
        out_shape=jax.ShapeDtypeStruct((t, MLA_HEADS * QK_CAT), BF16),
        compiler_params=_params(("parallel",)))(hmat, g_q.reshape(1, -1), wuq, wuk_t, sel, cos_q, sin_q)


def _mla_kv_kernel(ckv_ref, kr_ref, g_ref, c_ref, s1_ref, s2_ref, ckv_o, kr_o, kcat_o):
    ckv_n = _rms(ckv_ref[...], g_ref[...])
    ckv_o[...] = ckv_n
    kr = kr_ref[...]
    rot = kr * c_ref[...] + pltpu.roll(kr, 96, 1) * s1_ref[...] + pltpu.roll(kr, 32, 1) * s2_ref[...]
    kr_o[...] = rot[:, :ROPE]
    kcat_o[:, :KV_LORA] = ckv_n.astype(BF16)
    kcat_o[:, KV_LORA:] = rot.astype(BF16)


def mla_kv(hmat, ckv_col, kr_col, g_kv, c_tab, s1_tab, s2_tab, tm):
    t = hmat.shape[0]
    row = lambda i: (i, 0)
    return pl.pallas_call(
        _mla_kv_kernel, grid=(t // tm,),
        in_specs=[pl.BlockSpec((tm, KV_LORA), lambda i: (i, ckv_col)),
                  pl.BlockSpec((tm, LANES), lambda i: (i, kr_col)),
                  pl.BlockSpec((1, KV_LORA), lambda i: (0, 0)),
                  pl.BlockSpec((tm, LANES), row), pl.BlockSpec((tm, LANES), row), pl.BlockSpec((tm, LANES), row)],
        out_specs=[pl.BlockSpec((tm, KV_LORA), row), pl.BlockSpec((tm, ROPE), row), pl.BlockSpec((tm, QK_CAT), row)],
        out_shape=[jax.ShapeDtypeStruct((t, KV_LORA), F32), jax.ShapeDtypeStruct((t, ROPE), F32),
                   jax.ShapeDtypeStruct((t, QK_CAT), BF16)],
        compiler_params=_params(("parallel",)))(hmat, hmat, g_kv.reshape(1, -1), c_tab, s1_tab, s2_tab)


def _softplus(z):
    return jnp.maximum(z, 0.0) + jnp.log(1.0 + jnp.exp(-jnp.abs(z)))


def _tri_sum_matrix(s):
    j = jnp.arange(s)
    upper = (j[:, None] > j[None, :]).astype(BF16)
    return jnp.concatenate([upper, jnp.ones((s, s), BF16)], axis=1)


def _sb_chunk(qs, ks, vs, w2, mask, st):
    s = w2.shape[0]
    r = qs[0].shape[0]
    n = ks[0].shape[0] // s
    zs, sps, pieces = [], [], []
    for q, k in zip(qs, ks):
        z = lax.dot_general(q, k, NT, preferred_element_type=F32)
        sp = _softplus(z)
        lf = -sp if mask is None else jnp.where(mask, -sp, 0.0)
        zs.append(z)
        sps.append(sp)
        pieces += [lf[:, i * s:(i + 1) * s] for i in range(n)]
    lf_rows = jnp.concatenate(pieces, axis=0)
    hi = lf_rows.astype(BF16)
    lo = (lf_rows - hi.astype(F32)).astype(BF16)
    half = lf_rows.shape[0]
    sums = jnp.dot(jnp.concatenate([hi, lo], axis=0), w2, preferred_element_type=F32)
    sums = sums[:half] + sums[half:]
    out = []
    for h in range(len(qs)):
        acc, carry = st[2 * h], st[2 * h + 1]
        afters = []
        for i in range(n):
            blk = sums[(h * n + i) * r:(h * n + i + 1) * r]
            afters.append(blk[:, :s] + carry)
            carry = carry + blk[:, s:]
        after = afters[0] if n == 1 else jnp.concatenate(afters, axis=1)
        w = jnp.exp(zs[h] - sps[h] + after)
        if mask is not None:
            w = jnp.where(mask, w, 0.0)
        acc = acc + jnp.dot(w.astype(BF16), vs[h], preferred_element_type=F32)
        out += [acc, carry]
    return tuple(out)


def _sb_prompt_tile(i, q_ref, k_ref, v_ref, w2_ref, o_ref, tq, tk):
    q = jnp.concatenate([q_ref[:, g * SB_HEAD_DIM:(g + 1) * SB_HEAD_DIM] for g in range(SB_GROUP)], axis=0)
    q = (q * SB_SCALE).astype(BF16)
    r = SB_GROUP * tq
    qpos = i * tq + lax.broadcasted_iota(I32, (r, tk), 0) % tq
    lane = lax.broadcasted_iota(I32, (r, tk), 1)
    w2 = w2_ref[...]
    n_all = (i * tq + tq - 1) // tk + 1
    n_full = (i * tq) // tk

    def block(j, st, masked):
        start = pl.multiple_of(j * tk, tk)
        k = k_ref[pl.ds(start, tk), :].astype(BF16)
        v = v_ref[pl.ds(start, tk), :].astype(BF16)
        mask = ((j * tk + lane) < qpos) if masked else None
        return _sb_chunk([q], [k], [v], w2, mask, st)

    st = (jnp.zeros((r, SB_HEAD_DIM), F32), jnp.zeros((r, tk), F32))
    st = lax.fori_loop(0, n_all - n_full, lambda it, st: block(n_all - 1 - it, st, True), st)
    acc, _ = lax.fori_loop(0, n_full, lambda it, st: block(n_full - 1 - it, st, False), st)
    for g in range(SB_GROUP):
        o_ref[:, g * SB_HEAD_DIM:(g + 1) * SB_HEAD_DIM] = acc[g * tq:(g + 1) * tq].astype(o_ref.dtype)


def _paged_pipeline(pt_ref, n_pages, ppc, copies_for, compute_chunk, state, side_work=None):
    b = pl.program_id(0)
    nc = n_pages // ppc
    total = pl.num_programs(0) * nc
    ahead = RING - 1

    def start_chunk(g, slot):
        for i in range(ppc):
            for k, cp in enumerate(copies_for(pt_ref[g * ppc + i], slot, i)):
                cp.start(priority=(i + k) % N_DMA_THREADS)

    def wait_chunk(slot):
        for i in range(ppc):
            for cp in copies_for(0, slot, i):
                cp.wait()

    @pl.when(b == 0)
    def _():
        for g in range(ahead):
            @pl.when(g < total)
            def _():
                start_chunk(g, g % RING)

    if side_work is not None:
        side_work()

    def body(c, st):
        g = b * nc + c
        slot = g & (RING - 1)

        @pl.when(g + ahead < total)
        def _():
            start_chunk(g + ahead, (g + ahead) & (RING - 1))

        wait_chunk(slot)
        return compute_chunk(slot, st)

    return lax.fori_loop(0, nc, body, state)


def _sb_kernel(pt_ref, q_ref, kn_ref, vn_ref, w2_ref, pq_ref, pk_ref, pv_ref, pw2_ref, ck_hbm, cv_hbm, o_ref, po_ref,
               kbuf, vbuf, sem, *, n_pages, ppc, n_new, nq, tq, tk):
    dh = SB_HEAD_DIM
    rows = n_new * SB_GROUP
    page_rows = SB_KV_HEADS * PAGE_SIZE
    q = (q_ref[0] * SB_SCALE).astype(BF16)
    qs = [q[h] for h in range(SB_KV_HEADS)]
    w2 = w2_ref[...]

    def heads(ref, start, n_keys):
        return [ref[pl.ds(start + h, n_keys, stride=SB_KV_HEADS), :].astype(BF16) for h in range(SB_KV_HEADS)]

    tpos = lax.broadcasted_iota(I32, (rows, PAGE_SIZE), 0) // SB_GROUP
    mask = lax.broadcasted_iota(I32, (rows, PAGE_SIZE), 1) < tpos
    zero = (jnp.zeros((rows, dh), F32), jnp.zeros((rows, PAGE_SIZE), F32))
    st = _sb_chunk(qs, heads(kn_ref.at[0], 0, PAGE_SIZE), heads(vn_ref.at[0], 0, PAGE_SIZE), w2, mask,
                   zero * SB_KV_HEADS)

    def copies_for(page_id, slot, i):
        dst = pl.ds((slot * ppc + i) * page_rows, page_rows)
        return (pltpu.make_async_copy(ck_hbm.at[page_id], kbuf.at[dst], sem.at[0, slot]),
                pltpu.make_async_copy(cv_hbm.at[page_id], vbuf.at[dst], sem.at[1, slot]))

    def compute_chunk(slot, st):
        start = slot * ppc * page_rows
        n_keys = ppc * PAGE_SIZE
        return _sb_chunk(qs, heads(kbuf, start, n_keys), heads(vbuf, start, n_keys), w2, None, st)

    def prompt_tile():
        _sb_prompt_tile(pl.program_id(0) % nq, pq_ref, pk_ref, pv_ref, pw2_ref, po_ref, tq, tk)

    st = _paged_pipeline(pt_ref, n_pages, ppc, copies_for, compute_chunk, st, side_work=prompt_tile)
    for h in range(SB_KV_HEADS):
        o_ref[0, h] = st[2 * h]


def sb_attention(pt_rev, q, k_new, v_new, cache_k, cache_v, ppc, hmat, batch, seq, tk):
    db, kvh, rows, dh = q.shape
    n_pages = pt_rev.shape[0] // db
    page_rows = kvh * PAGE_SIZE
    tq = batch * seq * kvh // db
    nq = seq // tq
    assert tq % 8 == 0 and nq * tq == seq and batch * kvh * nq == db, (batch, seq, db)
    gw = SB_GROUP * dh
    k_col0 = SB_HEADS
    v_col0 = k_col0 + kvh
    p_tile = lambda b, pt: ((b // (kvh * nq)) * nq + b % nq, (b // nq) % kvh)
    grid_spec = pltpu.PrefetchScalarGridSpec(
        num_scalar_prefetch=1, grid=(db,),
        in_specs=[pl.BlockSpec((1, kvh, rows, dh), lambda b, pt: (b, 0, 0, 0)),
                  pl.BlockSpec((1, page_rows, dh), lambda b, pt: (b, 0, 0)),
                  pl.BlockSpec((1, page_rows, dh), lambda b, pt: (b, 0, 0)),
                  pl.BlockSpec((PAGE_SIZE, 2 * PAGE_SIZE), lambda b, pt: (0, 0)),
                  pl.BlockSpec((tq, gw), p_tile),
                  pl.BlockSpec((seq, dh), lambda b, pt: (b // (kvh * nq), k_col0 + (b // nq) % kvh)),
                  pl.BlockSpec((seq, dh), lambda b, pt: (b // (kvh * nq), v_col0 + (b // nq) % kvh)),
                  pl.BlockSpec((tk, 2 * tk), lambda b, pt: (0, 0)),
                  pl.BlockSpec(memory_space=pl.ANY), pl.BlockSpec(memory_space=pl.ANY)],
        out_specs=[pl.BlockSpec((1, kvh, rows, dh), lambda b, pt: (b, 0, 0, 0)), pl.BlockSpec((tq, gw), p_tile)],
        scratch_shapes=[pltpu.VMEM((RING * ppc * page_rows, dh), F32), pltpu.VMEM((RING * ppc * page_rows, dh), F32),
                        pltpu.SemaphoreType.DMA((2, RING))])
    return pl.pallas_call(
        functools.partial(_sb_kernel, n_pages=n_pages, ppc=ppc, n_new=rows // SB_GROUP, nq=nq, tq=tq, tk=tk),
        grid_spec=grid_spec,
        out_shape=[jax.ShapeDtypeStruct(q.shape, F32), jax.ShapeDtypeStruct((batch * seq, SB_HEADS * dh), BF16)],
        compiler_params=_params(("arbitrary",)))(pt_rev, q, k_new, v_new, _tri_sum_matrix(PAGE_SIZE), hmat, hmat, hmat,
                                                 _tri_sum_matrix(tk), cache_k, cache_v)


def _softmax_update(s, v, m, l, acc):
    m_new = jnp.maximum(m, jnp.max(s, axis=-1, keepdims=True))
    alpha = jnp.exp(m - m_new)
    p = jnp.exp(s - m_new)
    l = l * alpha + jnp.sum(p, axis=-1, keepdims=True)
    acc = acc * alpha + jnp.dot(p.astype(BF16), v, preferred_element_type=F32)
    return m_new, l, acc


def _mla_prompt_tile(i, q_ref, k_ref, wuv_ref, o_ref, qs_ref, acc_ref, m_ref, l_ref, tq, tk):
    rows = MLA_HEADS * tq
    for h in range(MLA_HEADS):
        qs_ref[h * tq:(h + 1) * tq, :] = q_ref[:, h * QK_CAT:(h + 1) * QK_CAT]
    acc_ref[...] = jnp.zeros(acc_ref.shape, F32)
    m_ref[...] = jnp.full(m_ref.shape, NEG, F32)
    l_ref[...] = jnp.zeros(l_ref.shape, F32)

    def step(j, masked):
        k = k_ref[pl.ds(pl.multiple_of(j * tk, tk), tk), :]
        s = lax.dot_general(qs_ref[...], k, NT, preferred_element_type=F32)
        if masked:
            qpos = i * tq + lax.broadcasted_iota(I32, (rows, tk), 0) % tq
            s = jnp.where(j * tk + lax.broadcasted_iota(I32, (rows, tk), 1) <= qpos, s, NEG)
        m_prev = m_ref[...]
        m_new = jnp.maximum(m_prev, jnp.max(s, axis=-1, keepdims=True))
        alpha = jnp.exp(m_prev - m_new)
        p = jnp.exp(s - _lanes(m_new, tk))
        l_ref[...] = alpha * l_ref[...] + jnp.sum(p, axis=-1, keepdims=True)
        m_ref[...] = m_new
        acc_ref[...] = acc_ref[...] * _lanes(alpha, KV_LORA) + jnp.dot(p.astype(BF16), k[:, :KV_LORA],
                                                                    preferred_element_type=F32)

    n_full = (i * tq + 1) // tk
    n_all = (i * tq + tq - 1) // tk + 1

    def full_body(j, c):
        step(j, False)
        return c

    def edge_body(j, c):
        step(j, True)
        return c

    lax.fori_loop(0, n_full, full_body, 0)
    lax.fori_loop(n_full, n_all, edge_body, 0)
    for h in range(MLA_HEADS):
        sl = slice(h * tq, (h + 1) * tq)
        lat = (acc_ref[sl, :] / _lanes(l_ref[sl, :], KV_LORA)).astype(BF16)
        o_ref[:, h * V_DIM:(h + 1) * V_DIM] = jnp.dot(lat, wuv_ref[h], preferred_element_type=F32).astype(o_ref.dtype)


def _mla_kernel(pt_ref, q_ref, kn_ref, pq_ref, pk_ref, wuv_ref, c_hbm, r_hbm, o_ref, po_ref, cbuf, rbuf, sem,
                qs_ref, acc_ref, m_ref, l_ref, *, n_pages, ppc, n_new, tq, tk):
    rows = n_new * MLA_HEADS
    q = q_ref[0]
    q_lat, q_rope = q[:, :KV_LORA], q[:, KV_LORA:]
    kn = kn_ref[0]
    tpos = lax.broadcasted_iota(I32, (rows, PAGE_SIZE), 0) // MLA_HEADS
    mask = lax.broadcasted_iota(I32, (rows, PAGE_SIZE), 1) <= tpos
    s_new = jnp.where(mask, lax.dot_general(q, kn, NT, preferred_element_type=F32), NEG)
    init = (jnp.full((rows, 1), NEG, F32), jnp.zeros((rows, 1), F32), jnp.zeros((rows, KV_LORA), F32))
    st = _softmax_update(s_new, kn[:, :KV_LORA], *init)

    def copies_for(page_id, slot, i):
        return (pltpu.make_async_copy(c_hbm.at[page_id], cbuf.at[slot, pl.ds(i * PAGE_SIZE, PAGE_SIZE)], sem.at[0, slot]),
                pltpu.make_async_copy(r_hbm.at[page_id], rbuf.at[slot, i], sem.at[1, slot]))

    n_keys = ppc * PAGE_SIZE
    rope_pad = jnp.zeros((LANES - ROPE, n_keys), BF16)

    def compute_chunk(slot, st):
        c = cbuf[slot].astype(BF16)
        krt = jnp.concatenate([rbuf[slot, i] for i in range(ppc)], axis=1).astype(BF16)
        krt = jnp.concatenate([krt, rope_pad], axis=0)
        s = (lax.dot_general(q_lat, c, NT, preferred_element_type=F32)
             + jnp.dot(q_rope, krt, preferred_element_type=F32))
        return _softmax_update(s, c, *st)

    def prompt_tile():
        nq = pk_ref.shape[0] // tq
        _mla_prompt_tile(pl.program_id(0) % nq, pq_ref, pk_ref, wuv_ref, po_ref, qs_ref, acc_ref, m_ref, l_ref, tq, tk)

    _, l, acc = _paged_pipeline(pt_ref, n_pages, ppc, copies_for, compute_chunk, st, side_work=prompt_tile)
    o_ref[0] = (acc / l).astype(o_ref.dtype)


def mla_attention(pt, q, k_new, cache_ckv, cache_krt, ppc, qcat, kcat, wuv, batch, seq, tk):
    db, rows, _ = q.shape
    n_pages = pt.shape[0] // db
    tq = batch * seq // db
    nq = seq // tq
    assert tq % 16 == 0 and nq * tq == seq and batch * nq == db, (batch, seq, db)
    prow = MLA_HEADS * tq
    p_tile = lambda b, pt: (b, 0)
    grid_spec = pltpu.PrefetchScalarGridSpec(
        num_scalar_prefetch=1, grid=(db,),
        in_specs=[pl.BlockSpec((1, rows, QK_CAT), lambda b, pt: (b, 0, 0)),
                  pl.BlockSpec((1, PAGE_SIZE, QK_CAT), lambda b, pt: (b, 0, 0)),
                  pl.BlockSpec((tq, MLA_HEADS * QK_CAT), p_tile),
                  pl.BlockSpec((seq, QK_CAT), lambda b, pt: (b // nq, 0)),
                  pl.BlockSpec(wuv.shape, lambda b, pt: (0, 0, 0)),
                  pl.BlockSpec(memory_space=pl.ANY), pl.BlockSpec(memory_space=pl.ANY)],
        out_specs=[pl.BlockSpec((1, rows, KV_LORA), lambda b, pt: (b, 0, 0)),
                   pl.BlockSpec((tq, MLA_HEADS * V_DIM), p_tile)],
        scratch_shapes=[pltpu.VMEM((RING, ppc * PAGE_SIZE, KV_LORA), F32), pltpu.VMEM((RING, ppc, ROPE, PAGE_SIZE), F32),
                        pltpu.SemaphoreType.DMA((2, RING)),
                        pltpu.VMEM((prow, QK_CAT), BF16), pltpu.VMEM((prow, KV_LORA), F32),
                        pltpu.VMEM((prow, LANES), F32), pltpu.VMEM((prow, LANES), F32)])
    return pl.pallas_call(
        functools.partial(_mla_kernel, n_pages=n_pages, ppc=ppc, n_new=rows // MLA_HEADS, tq=tq, tk=tk),
        grid_spec=grid_spec,
        out_shape=[jax.ShapeDtypeStruct((db, rows, KV_LORA), BF16),
                   jax.ShapeDtypeStruct((batch * seq, MLA_HEADS * V_DIM), BF16)],
        compiler_params=_params(("arbitrary",)))(pt, q, k_new, qcat, kcat, wuv, cache_ckv, cache_krt)


def _head_mm_kernel(a_ref, w_ref, o_ref):
    o_ref[...] = jnp.dot(a_ref[...], w_ref[0], preferred_element_type=F32).astype(o_ref.dtype)


def head_matmul(a, w, out_dtype):
    m = a.shape[0]
    h, k, n = w.shape
    return pl.pallas_call(
        _head_mm_kernel, grid=(h,),
        in_specs=[pl.BlockSpec((m, k), lambda i: (0, i)), pl.BlockSpec((1, k, n), lambda i: (i, 0, 0))],
        out_specs=pl.BlockSpec((m, n), lambda i: (0, i)),
        out_shape=jax.ShapeDtypeStruct((m, h * n), out_dtype),
        compiler_params=_params(("parallel",)))(a, w)


MEM_PAIR = 2


def _mem_attend(q, k, v):
    s = lax.dot_general((q * MEM_SCALE).astype(BF16), k.astype(BF16), NT, preferred_element_type=F32)
    p = jnp.exp(s - jnp.max(s, axis=-1, keepdims=True))
    o = jnp.dot(p.astype(BF16), v.astype(BF16), preferred_element_type=F32)
    return o / jnp.sum(p, axis=-1, keepdims=True)


def _mem_prompt_kernel(q_ref, k_ref, v_ref, o_ref):
    for h in range(MEM_PAIR):
        sl = slice(h * MEM_DIM, (h + 1) * MEM_DIM)
        o_ref[:, sl] = _mem_attend(q_ref[:, sl], k_ref[:, sl], v_ref[:, sl]).astype(o_ref.dtype)


def mem_attn_prompt(hmat, memkv, q_col0, batch, seq, n_mem, tq):
    nq = seq // tq
    w = MEM_PAIR * MEM_DIM
    n_pair = MEM_HEADS // MEM_PAIR
    return pl.pallas_call(
        _mem_prompt_kernel, grid=(batch, n_pair, nq),
        in_specs=[pl.BlockSpec((tq, w), lambda b, hp, i: (b * nq + i, q_col0 + hp)),
                  pl.BlockSpec((n_mem, w), lambda b, hp, i: (b, hp)),
                  pl.BlockSpec((n_mem, w), lambda b, hp, i: (b, n_pair + hp))],
        out_specs=pl.BlockSpec((tq, w), lambda b, hp, i: (b * nq + i, hp)),
        out_shape=jax.ShapeDtypeStruct((batch * seq, MEM_HEADS * MEM_DIM), BF16),
        compiler_params=_params(("parallel", "parallel", "arbitrary")))(hmat, memkv, memkv)


def _mem_sample_kernel(qa_ref, qb_ref, k_ref, v_ref, o_ref, *, n_new, group, n_mem):
    parts = MEM_DIM // LANES
    stride = MEM_HEADS * parts
    q = jnp.concatenate([qa_ref[...], qb_ref[...]], axis=-1)
    row = lax.broadcasted_iota(I32, (q.shape[0], MEM_DIM), 0) // n_new

    def head(ref, g, h):
        return jnp.concatenate([ref[g, pl.ds(p * MEM_HEADS + h, n_mem, stride=stride), :] for p in range(parts)], axis=-1)

    for h in range(MEM_HEADS):
        qh = q[:, h * MEM_DIM:(h + 1) * MEM_DIM]
        out = jnp.zeros(qh.shape, F32)
        for g in range(group):
            out = jnp.where(row == g, _mem_attend(qh, head(k_ref, g, h), head(v_ref, g, h)), out)
        o_ref[:, h * MEM_DIM:(h + 1) * MEM_DIM] = out.astype(o_ref.dtype)


def mem_sample(hmat, cache_k, cache_v, q_col0, row0, n_new, n_mem):
    db, cache_rows, _ = cache_k.shape
    group = 8 // n_new
    rows = group * n_new
    w = MEM_PAIR * MEM_DIM
    r0 = row0 // rows
    return pl.pallas_call(
        functools.partial(_mem_sample_kernel, n_new=n_new, group=group, n_mem=n_mem), grid=(db // group,),
        in_specs=[pl.BlockSpec((rows, w), lambda i: (r0 + i, q_col0)),
                  pl.BlockSpec((rows, w), lambda i: (r0 + i, q_col0 + 1)),
                  pl.BlockSpec((group, cache_rows, LANES), lambda i: (i, 0, 0)),
                  pl.BlockSpec((group, cache_rows, LANES), lambda i: (i, 0, 0))],
        out_specs=pl.BlockSpec((rows, MEM_HEADS * MEM_DIM), lambda i: (i, 0)),
        out_shape=jax.ShapeDtypeStruct((db * n_new, MEM_HEADS * MEM_DIM), BF16),
        compiler_params=_params(("arbitrary",)))(hmat, hmat, cache_k, cache_v)


def _mix_kernel(osb_ref, omla_ref, omem_ref, wsb_ref, wmla_ref, wmem_ref, gsb_ref, gmla_ref, gmem_ref, o_ref):
    mix = jnp.zeros(o_ref.shape, F32)
    for o, w, g in ((osb_ref, wsb_ref, gsb_ref), (omla_ref, wmla_ref, gmla_ref), (omem_ref, wmem_ref, gmem_ref)):
        mix = mix + jax.nn.sigmoid(g[...]) * jnp.dot(o[...], w[...], preferred_element_type=F32)
    o_ref[...] = mix.astype(o_ref.dtype)


def branch_mix(o_sb, o_mla, o_mem, w_sb, w_mla, w_mem, hmat, gate_col0, d, tm, tn):
    t, kdim = o_sb.shape
    nn = d // tn
    a_spec = pl.BlockSpec((tm, kdim), lambda i, j: (i, 0))
    w_spec = pl.BlockSpec((kdim, tn), lambda i, j: (0, j))
    g_specs = [pl.BlockSpec((tm, tn), lambda i, j, br=br: (i, gate_col0 + br * nn + j)) for br in range(3)]
    return pl.pallas_call(
        _mix_kernel, grid=(t // tm, nn),
        in_specs=[a_spec, a_spec, a_spec, w_spec, w_spec, w_spec] + g_specs,
        out_specs=pl.BlockSpec((tm, tn), lambda i, j: (i, j)),
        out_shape=jax.ShapeDtypeStruct((t, d), BF16),
        compiler_params=_params(("parallel", "arbitrary")))(o_sb, o_mla, o_mem, w_sb, w_mla, w_mem, hmat, hmat, hmat)


def _out_proj_kernel(mix_ref, w_ref, x_ref, g_ref, rhi_ref, rlo_ref, b_ref, xmid_ref, hn_ref, lg_ref):
    xmid = x_ref[...] + jnp.dot(mix_ref[...], w_ref[...], preferred_element_type=F32)
    xmid_ref[...] = xmid
    hn = _rms(xmid, g_ref[...])
    tm, d = hn.shape
    spr = d // LANES
    for c in range(spr):
        hn_ref[pl.ds(c, tm, stride=spr), :] = hn[:, c * LANES:(c + 1) * LANES]
    hi = hn.astype(BF16)
    lo = (hn - hi.astype(F32)).astype(BF16)
    lg = (jnp.dot(hi, rhi_ref[...], preferred_element_type=F32) + jnp.dot(lo, rhi_ref[...], preferred_element_type=F32)
          + jnp.dot(hi, rlo_ref[...], preferred_element_type=F32))
    lg_ref[...] = lg + b_ref[...]


def out_proj(mix, w_out, x, g_ffn, r_hi, r_lo, r_bias, tm):
    t, d = x.shape
    row = lambda i: (i, 0)
    fix = lambda i: (0, 0)
    return pl.pallas_call(
        _out_proj_kernel, grid=(t // tm,),
        in_specs=[pl.BlockSpec((tm, d), row), pl.BlockSpec((d, d), fix), pl.BlockSpec((tm, d), row),
                  pl.BlockSpec((1, d), fix), pl.BlockSpec((d, LANES), fix), pl.BlockSpec((d, LANES), fix),
                  pl.BlockSpec((1, LANES), fix)],
        out_specs=[pl.BlockSpec((tm, d), row), pl.BlockSpec((tm * (d // LANES), LANES), row),
                   pl.BlockSpec((tm, LANES), row)],
        out_shape=[jax.ShapeDtypeStruct((t, d), F32), jax.ShapeDtypeStruct((t * (d // LANES), LANES), F32),
                   jax.ShapeDtypeStruct((t, LANES), F32)],
        compiler_params=_params(("parallel",)))(mix, w_out, x, g_ffn.reshape(1, d), r_hi, r_lo, r_bias)


def _route_kernel(lg_ref, o_ref):
    lg = lg_ref[...]
    lane = lax.broadcasted_iota(I32, lg.shape, 1)
    lanef = lane.astype(F32)

    def first_max(x):
        v = jnp.max(x, axis=-1, keepdims=True)
        return v, jnp.min(jnp.where(x == v, lanef, 1e6), axis=-1, keepdims=True)

    is_grp = lane < N_GROUPS
    gl = jnp.where(is_grp, lg, NEG)
    gmax, g_idx = first_max(gl)
    g_p = 1.0 / jnp.sum(jnp.where(is_grp, jnp.exp(gl - gmax), 0.0), axis=-1, keepdims=True)
    grp_of_lane = ((lane - N_GROUPS) >> 3).astype(F32)
    el = jnp.where((lane >= N_GROUPS) & (grp_of_lane == g_idx), lg, NEG)
    v1, i1 = first_max(el)
    v2, i2 = first_max(jnp.where(lanef == i1, NEG, el))
    d = jnp.exp(v2 - v1)
    w1 = g_p / (1.0 + d)
    w2 = g_p * d / (1.0 + d)
    e1 = i1 - N_GROUPS
    e2 = i2 - N_GROUPS
    o_ref[...] = jnp.where(lane == 0, e1, jnp.where(lane == 1, e2, jnp.where(lane == 2, w1, jnp.where(lane == 3, w2, 0.0))))


def route(logits, tm):
    t = logits.shape[0]
    return pl.pallas_call(
        _route_kernel, grid=(t // tm,),
        in_specs=[pl.BlockSpec((tm, LANES), lambda i: (i, 0))], out_specs=pl.BlockSpec((tm, LANES), lambda i: (i, 0)),
        out_shape=jax.ShapeDtypeStruct((t, LANES), F32), compiler_params=_params(("parallel",)))(logits)


def _slab_copies(idx_ref, base, n, src_hbm, dst, dst_row0, spr, sem):
    cps = []
    for r in range(n):
        src_row = pl.multiple_of(idx_ref[base + r] * spr, spr)
        cps.append(pltpu.make_async_copy(src_hbm.at[pl.ds(src_row, spr)], dst.at[pl.ds(dst_row0 + r * spr, spr)], sem))
    return cps


def _expert_kernel(tok_ref, be_ref, first_ref, wslot_ref, nexte_ref, nused_ref, h_hbm, wg_hbm, wu_hbm, wd_hbm, o_ref,
                   xbuf, wg_f32, wu_f32, wd_f32, wg_bf, wu_bf, wd_bf, xsem, wsem, *, spr):
    u = pl.program_id(0)
    n_used = nused_ref[0]
    blk_rows = MOE_BLOCK * spr

    def gather(blk, slot):
        return _slab_copies(tok_ref, blk * MOE_BLOCK, MOE_BLOCK, h_hbm, xbuf, slot * blk_rows, spr, xsem.at[slot])

    def weights(e, slot):
        return (pltpu.make_async_copy(wg_hbm.at[e], wg_f32.at[slot], wsem.at[slot]),
                pltpu.make_async_copy(wu_hbm.at[e], wu_f32.at[slot], wsem.at[slot]),
                pltpu.make_async_copy(wd_hbm.at[e], wd_f32.at[slot], wsem.at[slot]))

    @pl.when(u == 0)
    def _():
        for cp in weights(be_ref[0], 0):
            cp.start(priority=1)
        for cp in gather(0, 0):
            cp.start()

    @pl.when(u + 1 < n_used)
    def _():
        for cp in gather(u + 1, (u + 1) & 1):
            cp.start()

    @pl.when(u < n_used)
    def _():
        @pl.when(first_ref[u] == 1)
        def _():
            ws = wslot_ref[u]
            nxt = nexte_ref[u]

            @pl.when(nxt >= 0)
            def _():
                for cp in weights(nxt, 1 - ws):
                    cp.start(priority=1)

            for cp in weights(0, ws):
                cp.wait()
            wg_bf[...] = wg_f32[ws].astype(BF16)
            wu_bf[...] = wu_f32[ws].astype(BF16)
            wd_bf[...] = wd_f32[ws].astype(BF16)

        slot = u & 1
        for cp in gather(u, slot):
            cp.wait()
        x = jnp.concatenate([xbuf[pl.ds(slot * blk_rows + c, MOE_BLOCK, stride=spr), :] for c in range(spr)], axis=-1)
        x = x.astype(BF16)
        gate = jnp.dot(x, wg_bf[...], preferred_element_type=F32)
        up = jnp.dot(x, wu_bf[...], preferred_element_type=F32)
        a = (gate * jax.nn.sigmoid(gate) * up).astype(BF16)
        y = jnp.dot(a, wd_bf[...], preferred_element_type=F32)
        for c in range(spr):
            o_ref[pl.ds(c, MOE_BLOCK, stride=spr), :] = y[:, c * LANES:(c + 1) * LANES]

    @pl.when(u >= n_used)
    def _():
        o_ref[...] = jnp.zeros(o_ref.shape, o_ref.dtype)


def moe_experts(slot_tok, plan, h_slabs, w_gate, w_up, w_down):
    n_slots = slot_tok.shape[0]
    _, d, de = w_gate.shape
    spr = d // LANES
    grid_spec = pltpu.PrefetchScalarGridSpec(
        num_scalar_prefetch=6, grid=(n_slots // MOE_BLOCK,),
        in_specs=[pl.BlockSpec(memory_space=pl.ANY)] * 4,
        out_specs=pl.BlockSpec((MOE_BLOCK * spr, LANES), lambda u, *_: (u, 0)),
        scratch_shapes=[pltpu.VMEM((2 * MOE_BLOCK * spr, LANES), F32),
                        pltpu.VMEM((2, d, de), F32), pltpu.VMEM((2, d, de), F32), pltpu.VMEM((2, de, d), F32),
                        pltpu.VMEM((d, de), BF16), pltpu.VMEM((d, de), BF16), pltpu.VMEM((de, d), BF16),
                        pltpu.SemaphoreType.DMA((2,)), pltpu.SemaphoreType.DMA((2,))])
    return pl.pallas_call(functools.partial(_expert_kernel, spr=spr), grid_spec=grid_spec,
                          out_shape=jax.ShapeDtypeStruct((n_slots * spr, LANES), F32),
                          compiler_params=_params(("arbitrary",)))(slot_tok, *plan, h_slabs, w_gate, w_up, w_down)


def _combine_kernel(s1_ref, s2_ref, x_ref, rt_ref, g_ref, y_hbm, o_ref, ybuf, sem, *, tm, spr):
    i = pl.program_id(0)
    n = pl.num_programs(0)
    half_rows = tm * spr

    def gather(step, slot):
        base = step * tm
        return (_slab_copies(s1_ref, base, tm, y_hbm, ybuf, (2 * slot) * half_rows, spr, sem.at[slot])
                + _slab_copies(s2_ref, base, tm, y_hbm, ybuf, (2 * slot + 1) * half_rows, spr, sem.at[slot]))

    @pl.when(i == 0)
    def _():
        for k, cp in enumerate(gather(0, 0)):
            cp.start(priority=k % N_DMA_THREADS)

    @pl.when(i + 1 < n)
    def _():
        for k, cp in enumerate(gather(i + 1, (i + 1) & 1)):
            cp.start(priority=k % N_DMA_THREADS)

    slot = i & 1
    for cp in gather(i, slot):
        cp.wait()
    rt = rt_ref[...]
    w1, w2 = rt[:, 2:3], rt[:, 3:4]
    ss = jnp.zeros((tm, 1), F32)
    for c in range(spr):
        sl = slice(c * LANES, (c + 1) * LANES)
        y1 = ybuf[pl.ds((2 * slot) * half_rows + c, tm, stride=spr), :]
        y2 = ybuf[pl.ds((2 * slot + 1) * half_rows + c, tm, stride=spr), :]
        xc = x_ref[:, sl] + (w1 * y1 + w2 * y2)
        ss = ss + jnp.sum(xc * xc, axis=-1, keepdims=True)
        o_ref[:, sl] = xc
    scale = lax.rsqrt(ss / (spr * LANES) + NORM_EPS)
    o_ref[...] = o_ref[...] * scale * g_ref[...]


def moe_combine(slot1, slot2, xmid, routed, g_final, y_slabs, tm):
    t, d = xmid.shape
    spr = d // LANES
    grid_spec = pltpu.PrefetchScalarGridSpec(
        num_scalar_prefetch=2, grid=(t // tm,),
        in_specs=[pl.BlockSpec((tm, d), lambda i, a, b: (i, 0)), pl.BlockSpec((tm, LANES), lambda i, a, b: (i, 0)),
                  pl.BlockSpec((1, d), lambda i, a, b: (0, 0)), pl.BlockSpec(memory_space=pl.ANY)],
        out_specs=pl.BlockSpec((tm, d), lambda i, a, b: (i, 0)),
        scratch_shapes=[pltpu.VMEM((4 * tm * spr, LANES), F32), pltpu.SemaphoreType.DMA((2,))])
    return pl.pallas_call(functools.partial(_combine_kernel, tm=tm, spr=spr), grid_spec=grid_spec,
                          out_shape=jax.ShapeDtypeStruct((t, d), F32),
                          compiler_params=_params(("arbitrary",)))(slot1, slot2, xmid, routed, g_final.reshape(1, d), y_slabs)


def _dispatch_plan(e12, n_exp):
    t = e12.shape[0]
    flat_e = e12.reshape(-1)
    tk = flat_e.shape[0]
    n_blocks = -(-tk // MOE_BLOCK) + n_exp
    order = jnp.argsort(flat_e).astype(I32)
    se = flat_e[order]
    counts = jnp.bincount(flat_e, length=n_exp).astype(I32)
    start = jnp.cumsum(counts) - counts
    padded = (counts + MOE_BLOCK - 1) // MOE_BLOCK * MOE_BLOCK
    pad_end = jnp.cumsum(padded)
    dest = (pad_end - padded)[se] + jnp.arange(tk, dtype=I32) - start[se]
    slot_of = jnp.zeros((tk,), I32).at[order].set(dest).reshape(t, 2)
    slot_tok = jnp.zeros((n_blocks * MOE_BLOCK,), I32).at[dest].set(order // 2)
    blocks = jnp.arange(n_blocks, dtype=I32)
    block_e = jnp.minimum(jnp.searchsorted(pad_end, blocks * MOE_BLOCK, side='right'), n_exp - 1).astype(I32)
    n_used = (pad_end[-1:] // MOE_BLOCK).astype(I32)
    ids = jnp.arange(n_exp, dtype=I32)
    busy = counts > 0
    nxt = lax.cummin(jnp.where(busy, ids, n_exp)[::-1])[::-1]
    next_busy = jnp.concatenate([nxt[1:], jnp.full((1,), n_exp, I32)])
    run = jnp.cumsum(busy.astype(I32)) - 1
    first = ((blocks == ((pad_end - padded) // MOE_BLOCK)[block_e]) & (blocks < n_used[0])).astype(I32)
    wslot = run[block_e] & 1
    next_e = jnp.where(next_busy[block_e] < n_exp, next_busy[block_e], -1).astype(I32)
    return slot_of, slot_tok, (block_e, first, wslot, next_e, n_used)


def _tile(n, target, mult):
    best = None
    for c in range(mult, min(n, target) + 1, mult):
        if n % c == 0:
            best = c
    assert best is not None, (n, target, mult)
    return best


def kernel(x_prompt, x_sample, mem_prompt, cache_sb_k, cache_sb_v, cache_mla_ckv, cache_mla_krope, cache_mem_k, cache_mem_v, page_table, g_attn_norm, w_in, g_q_norm, w_uq, g_kv_norm, w_uk, w_uv, g_mem_norm, w_mem_kv, w_br_sb, w_br_mla, w_br_mem, w_out, g_ffn_norm, w_grp, b_grp, w_rtr, b_rtr, w_gate, w_up, w_down, g_final):
    depth = g_attn_norm.shape[0]
    assert depth == 1
    batch, seq, d = x_prompt.shape
    db, n_new, _ = x_sample.shape
    n_mem = mem_prompt.shape[1]
    n_pages = page_table.shape[1]
    n_pool = cache_sb_k.shape[1]
    tp, ts = batch * seq, db * n_new
    t = tp + ts
    l = 0

    sbq_w, sbkv_w = SB_HEADS * SB_HEAD_DIM, SB_KV_HEADS * SB_HEAD_DIM
    memq_w = MEM_HEADS * MEM_DIM
    c_kr = 2 * sbkv_w + sbq_w + Q_LORA + KV_LORA
    w = w_in[l]
    wp = jnp.concatenate([w[:, :c_kr], w[:, c_kr + ROPE:], w[:, c_kr:c_kr + ROPE], jnp.zeros((d, LANES - ROPE), F32)],
                         axis=1).astype(BF16)
    col_cq = sbq_w + 2 * sbkv_w
    col_ckv = col_cq + Q_LORA
    col_memq = col_ckv + KV_LORA
    col_gate = col_memq + memq_w
    col_kr = col_gate + 3 * d
    n_cols = wp.shape[1]

    half = ROPE // 2
    inv = ROPE_THETA ** (-jnp.arange(half, dtype=F32) / half)
    pos = jnp.concatenate([jnp.tile(jnp.arange(seq), batch), jnp.tile(n_pages * PAGE_SIZE + jnp.arange(n_new), db)])
    ang = pos.astype(F32)[:, None] * inv[None, :]
    cos, sin = jnp.cos(ang), jnp.sin(ang)
    zeros = jnp.zeros_like(cos)
    cos_q, sin_q = jnp.tile(cos, (1, MLA_HEADS)), jnp.tile(sin, (1, MLA_HEADS))
    c_tab = jnp.concatenate([cos, cos, zeros, zeros], axis=1)
    s1_tab = jnp.concatenate([-sin, zeros, zeros, zeros], axis=1)
    s2_tab = jnp.concatenate([zeros, sin, zeros, zeros], axis=1)

    wuq = w_uq[l]
    wuq2 = jnp.concatenate([wuq[:, :, :NOPE].reshape(Q_LORA, -1), wuq[:, :, NOPE:NOPE + half].reshape(Q_LORA, -1),
                            wuq[:, :, NOPE + half:].reshape(Q_LORA, -1)], axis=1).astype(BF16)
    wuk_t = jnp.transpose(w_uk[l], (1, 2, 0)).astype(BF16)
    wuv = jnp.transpose(w_uv[l], (1, 0, 2)).astype(BF16)
    rr = jnp.arange(2 * MLA_HEADS * half)[:, None]
    cc = jnp.arange(MLA_HEADS * LANES)[None, :]
    sel = (cc == (rr % (MLA_HEADS * half)) // half * LANES + rr // (MLA_HEADS * half) * half + rr % half).astype(BF16)

    xcat = jnp.concatenate([x_prompt.reshape(tp, d), x_sample.reshape(ts, d)], axis=0)
    tm_big = _tile(t, 1152, 16)
    hn_attn = rmsnorm(xcat, g_attn_norm[l], BF16, _tile(t, 640, 16))
    hmat = matmul(hn_attn, wp, F32, tm_big, _tile(n_cols, 1024, LANES))

    tm_mid = _tile(t, 640, 16)
    qcat = mla_q(hmat, col_cq // Q_LORA, g_q_norm[l], wuq2, wuk_t, sel, cos_q, sin_q, tm_mid)
    ckv_n, kr_rot, kcat = mla_kv(hmat, col_ckv // KV_LORA, col_kr // LANES, g_kv_norm[l], c_tab, s1_tab, s2_tab, tm_mid)

    memn = rmsnorm(mem_prompt.reshape(batch * n_mem, d), g_mem_norm[l], BF16, _tile(batch * n_mem, 512, 16))
    memkv = matmul(memn, w_mem_kv[l].astype(BF16), F32, _tile(batch * n_mem, 1024, 16), 1024)

    o_mem_p = mem_attn_prompt(hmat, memkv, col_memq // (MEM_PAIR * MEM_DIM), batch, seq, n_mem, _tile(seq, 512, 8))

    hs = hmat[tp:]
    rows_sb = n_new * SB_GROUP
    q_sb = hs[:, :sbq_w].reshape(db, n_new, SB_KV_HEADS, SB_GROUP, SB_HEAD_DIM)
    q_sb = jnp.transpose(q_sb, (0, 2, 1, 3, 4)).reshape(db, SB_KV_HEADS, rows_sb, SB_HEAD_DIM)
    page_rows = PAGE_SIZE * SB_KV_HEADS

    def new_kv(cols):
        x = hs[:, cols:cols + sbkv_w].reshape(db, n_new * SB_KV_HEADS, SB_HEAD_DIM)
        return jnp.pad(x, ((0, 0), (0, page_rows - n_new * SB_KV_HEADS), (0, 0)))

    pt = page_table.astype(I32)
    ppc = _tile(n_pages, 16, 1)
    o_sb_s, o_sb_p = sb_attention(pt[:, ::-1].reshape(-1), q_sb, new_kv(sbq_w), new_kv(sbq_w + sbkv_w),
                                  cache_sb_k[l].reshape(n_pool, page_rows, SB_HEAD_DIM),
                                  cache_sb_v[l].reshape(n_pool, page_rows, SB_HEAD_DIM), ppc, hmat, batch, seq,
                                  min(256, seq))
    o_sb_s = jnp.transpose(o_sb_s.reshape(db, SB_KV_HEADS, n_new, SB_GROUP, SB_HEAD_DIM), (0, 2, 1, 3, 4))
    o_sb_s = o_sb_s.reshape(ts, sbq_w).astype(BF16)

    q_mla_s = qcat[tp:].reshape(db, n_new * MLA_HEADS, QK_CAT)
    kcat_new = jnp.pad(kcat[tp:].reshape(db, n_new, QK_CAT), ((0, 0), (0, PAGE_SIZE - n_new), (0, 0)))
    lat_s, o_mla_p = mla_attention(pt.reshape(-1), q_mla_s, kcat_new, cache_mla_ckv[l],
                                   jnp.transpose(cache_mla_krope[l], (0, 2, 1)), ppc, qcat, kcat, wuv, batch, seq,
                                   min(512, seq))
    o_mla_s = head_matmul(lat_s.reshape(ts, MLA_HEADS * KV_LORA), wuv, BF16)

    def mem_rows(c):
        c = c.reshape(db, n_mem, MEM_HEADS, MEM_DIM // LANES, LANES)
        return jnp.transpose(c, (0, 1, 3, 2, 4)).reshape(db, n_mem * memq_w // LANES, LANES)

    o_mem_s = mem_sample(hmat, mem_rows(cache_mem_k[l]), mem_rows(cache_mem_v[l]),
                         col_memq // (MEM_PAIR * MEM_DIM), tp, n_new, n_mem)

    o_sb = jnp.concatenate([o_sb_p, o_sb_s], axis=0)
    o_mla = jnp.concatenate([o_mla_p, o_mla_s], axis=0)
    o_mem = jnp.concatenate([o_mem_p, o_mem_s], axis=0)
    tn_mix = 512
    mix = branch_mix(o_sb, o_mla, o_mem, w_br_sb[l].astype(BF16), w_br_mla[l].astype(BF16), w_br_mem[l].astype(BF16),
                     hmat, col_gate // tn_mix, d, tm_big, tn_mix)
    w_r = jnp.concatenate([w_grp[l], w_rtr[l], jnp.zeros((d, LANES - N_GROUPS - N_EXPERTS), F32)], axis=1)
    r_hi = w_r.astype(BF16)
    r_lo = (w_r - r_hi.astype(F32)).astype(BF16)
    r_bias = jnp.concatenate([b_grp[l], b_rtr[l], jnp.zeros((LANES - N_GROUPS - N_EXPERTS,), F32)]).reshape(1, LANES)
    xmid, hn_ffn, logits = out_proj(mix, w_out[l].astype(BF16), xcat, g_ffn_norm[l], r_hi, r_lo, r_bias, _tile(t, 320, 16))

    routed = route(logits, _tile(t, 1152, 8))
    slot_of, slot_tok, plan = _dispatch_plan(routed[:, :2].astype(I32), N_EXPERTS)
    y_slabs = moe_experts(slot_tok, plan, hn_ffn, w_gate[l], w_up[l], w_down[l])
    y = moe_combine(slot_of[:, 0], slot_of[:, 1], xmid, routed, g_final, y_slabs, _tile(t, 128, 8))

    def kv_out(cols, rows, shape):
        return rows[:, cols[0]:cols[1]].reshape(shape)[None]

    hp = hmat[:tp]
    return (y[:tp].reshape(batch, seq, d), y[tp:].reshape(db, n_new, d),
            kv_out((sbq_w, sbq_w + sbkv_w), hp, (batch, seq, SB_KV_HEADS, SB_HEAD_DIM)),
            kv_out((sbq_w + sbkv_w, sbq_w + 2 * sbkv_w), hp, (batch, seq, SB_KV_HEADS, SB_HEAD_DIM)),
            ckv_n[:tp].reshape(batch, seq, KV_LORA)[None], kr_rot[:tp].reshape(batch, seq, ROPE)[None],
            memkv[:, :memq_w].reshape(batch, n_mem, MEM_HEADS, MEM_DIM)[None],
            memkv[:, memq_w:].reshape(batch, n_mem, MEM_HEADS, MEM_DIM)[None],
            kv_out((sbq_w, sbq_w + sbkv_w), hs, (db, n_new, SB_KV_HEADS, SB_HEAD_DIM)),
            kv_out((sbq_w + sbkv_w, sbq_w + 2 * sbkv_w), hs, (db, n_new, SB_KV_HEADS, SB_HEAD_DIM)),
            ckv_n[tp:].reshape(db, n_new, KV_LORA)[None], kr_rot[tp:].reshape(db, n_new, ROPE)[None])
```

```python
import functools

import jax
import jax.numpy as jnp
from jax import lax
from jax.experimental import pallas as pl
from jax.experimental.pallas import tpu as pltpu

F32 = jnp.float32
BF16 = jnp.bfloat16
I32 = jnp.int32

NORM_EPS = 1e-6
ROPE_THETA = 10000.0
PAGE_SIZE = 128
LANES = 128
N_DMA_THREADS = 2
RING = 4
W_PARTS = 8
SB_HEADS, SB_KV_HEADS, SB_HEAD_DIM = 8, 2, 128
SB_GROUP = SB_HEADS // SB_KV_HEADS
SB_SCALE = SB_HEAD_DIM ** -0.5
MLA_HEADS, Q_LORA, KV_LORA, NOPE, ROPE, V_DIM = 8, 512, 512, 128, 64, 128
MLA_SCALE = (NOPE + ROPE) ** -0.5
QK_CAT = KV_LORA + LANES
MEM_HEADS, MEM_DIM = 4, 256
MEM_SCALE = MEM_DIM ** -0.5
N_GROUPS, EPG, N_EXPERTS, D_EXPERT = 8, 8, 64, 512
MOE_BLOCK = 128
NEG = -1e30
VMEM_LIMIT = 56 * 1024 * 1024
NT = (((1,), (1,)), ((), ()))


def _params(sem):
    return pltpu.CompilerParams(dimension_semantics=sem, vmem_limit_bytes=VMEM_LIMIT)


def _rms(x, g):
    return x * lax.rsqrt(jnp.mean(x * x, axis=-1, keepdims=True) + NORM_EPS) * g


def _lanes(x, n):
    return x if n == LANES else jnp.concatenate([x] * (n // LANES), axis=1)


def _rmsnorm_kernel(x_ref, g_ref, o_ref):
    o_ref[...] = _rms(x_ref[...], g_ref[...]).astype(o_ref.dtype)


def rmsnorm(x, g, out_dtype, tm):
    r, d = x.shape
    return pl.pallas_call(
        _rmsnorm_kernel, grid=(r // tm,),
        in_specs=[pl.BlockSpec((tm, d), lambda i: (i, 0)), pl.BlockSpec((1, d), lambda i: (0, 0))],
        out_specs=pl.BlockSpec((tm, d), lambda i: (i, 0)),
        out_shape=jax.ShapeDtypeStruct((r, d), out_dtype),
        compiler_params=_params(("parallel",)))(x, g.reshape(1, d))


def _mm_kernel(a_ref, b_ref, o_ref):
    o_ref[...] = jnp.dot(a_ref[...], b_ref[...], preferred_element_type=F32).astype(o_ref.dtype)


def matmul(a, b, out_dtype, tm, tn):
    m, k = a.shape
    n = b.shape[1]
    return pl.pallas_call(
        _mm_kernel, grid=(m // tm, n // tn),
        in_specs=[pl.BlockSpec((tm, k), lambda i, j: (i, 0)), pl.BlockSpec((k, tn), lambda i, j: (0, j))],
        out_specs=pl.BlockSpec((tm, tn), lambda i, j: (i, j)),
        out_shape=jax.ShapeDtypeStruct((m, n), out_dtype),
        compiler_params=_params(("parallel", "arbitrary")))(a, b)


def _mla_q_kernel(cq_ref, g_ref, wuq_ref, wuk_ref, sel_ref, cos_ref, sin_ref, o_ref):
    cqn = _rms(cq_ref[...], g_ref[...]).astype(BF16)
    q = jnp.dot(cqn, wuq_ref[...], preferred_element_type=F32) * MLA_SCALE
    n_np = MLA_HEADS * NOPE
    half = MLA_HEADS * (ROPE // 2)
    r1 = q[:, n_np:n_np + half]
    r2 = q[:, n_np + half:n_np + 2 * half]
    cos, sin = cos_ref[...], sin_ref[...]
    rot = jnp.concatenate([r1 * cos - r2 * sin, r1 * sin + r2 * cos], axis=-1).astype(BF16)
    tail = jnp.dot(rot, sel_ref[...], preferred_element_type=F32)
    for h in range(MLA_HEADS):
        nope = q[:, h * NOPE:(h + 1) * NOPE].astype(BF16)
        lat = jnp.dot(nope, wuk_ref[h], preferred_element_type=F32)
        o_ref[:, h * QK_CAT:h * QK_CAT + KV_LORA] = lat.astype(o_ref.dtype)
        o_ref[:, h * QK_CAT + KV_LORA:(h + 1) * QK_CAT] = tail[:, h * LANES:(h + 1) * LANES].astype(o_ref.dtype)


def mla_q(hmat, cq_col, g_q, wuq, wuk_t, sel, cos_q, sin_q, tm):
    t = hmat.shape[0]
    return pl.pallas_call(
        _mla_q_kernel, grid=(t // tm,),
        in_specs=[pl.BlockSpec((tm, Q_LORA), lambda i: (i, cq_col)),
                  pl.BlockSpec((1, Q_LORA), lambda i: (0, 0)),
                  pl.BlockSpec(wuq.shape, lambda i: (0, 0)),
                  pl.BlockSpec(wuk_t.shape, lambda i: (0, 0, 0)),
                  pl.BlockSpec(sel.shape, lambda i: (0, 0)),
                  pl.BlockSpec((tm, cos_q.shape[1]), lambda i: (i, 0)),
                  pl.BlockSpec((tm, sin_q.shape[1]), lambda i: (i, 0))],
        out_specs=pl.BlockSpec((tm, MLA_HEADS * QK_CAT), lambda i: (i, 0)),
        out_shape=jax.ShapeDtypeStruct((t, MLA_HEADS * QK_CAT), BF16),
        compiler_params=_params(("parallel",)))(hmat, g_q.reshape(1, -1), wuq, wuk_t, sel, cos_q, sin_q)


def _mla_kv_kernel(ckv_ref, kr_ref, g_ref, c_ref, s1_ref, s2_ref, ckv_o, kr_o, kcat_o):
    ckv_n = _rms(ckv_ref[...], g_ref[...])
    ckv_o[...] = ckv_n
    kr = kr_ref[...]
    rot = kr * c_ref[...] + pltpu.roll(kr, 96, 1) * s1_ref[...] + pltpu.roll(kr, 32, 1) * s2_ref[...]
    kr_o[...] = rot[:, :ROPE]
    kcat_o[:, :KV_LORA] = ckv_n.astype(BF16)
    kcat_o[:, KV_LORA:] = rot.astype(BF16)


def mla_kv(hmat, ckv_col, kr_col, g_kv, c_tab, s1_tab, s2_tab, tm):
    t = hmat.shape[0]
    row = lambda i: (i, 0)
    return pl.pallas_call(
        _mla_kv_kernel, grid=(t // tm,),
        in_specs=[pl.BlockSpec((tm, KV_LORA), lambda i: (i, ckv_col)),
                  pl.BlockSpec((tm, LANES), lambda i: (i, kr_col)),
                  pl.BlockSpec((1, KV_LORA), lambda i: (0, 0)),
                  pl.BlockSpec((tm, LANES), row), pl.BlockSpec((tm, LANES), row), pl.BlockSpec((tm, LANES), row)],
        out_specs=[pl.BlockSpec((tm, KV_LORA), row), pl.BlockSpec((tm, ROPE), row), pl.BlockSpec((tm, QK_CAT), row)],
        out_shape=[jax.ShapeDtypeStruct((t, KV_LORA), F32), jax.ShapeDtypeStruct((t, ROPE), F32),
                   jax.ShapeDtypeStruct((t, QK_CAT), BF16)],
        compiler_params=_params(("parallel",)))(hmat, hmat, g_kv.reshape(1, -1), c_tab, s1_tab, s2_tab)


def _softplus(z):
    return jnp.maximum(z, 0.0) + jnp.log(1.0 + jnp.exp(-jnp.abs(z)))


def _tri_sum_matrix(s):
    j = jnp.arange(s)
    upper = (j[:, None] > j[None, :]).astype(BF16)
    return jnp.concatenate([upper, jnp.ones((s, s), BF16)], axis=1)


def _sb_chunk(qs, ks, vs, w2, mask, st):
    s = w2.shape[0]
    r = qs[0].shape[0]
    n = ks[0].shape[0] // s
    zs, sps, pieces = [], [], []
    for q, k in zip(qs, ks):
        z = lax.dot_general(q, k, NT, preferred_element_type=F32)
        sp = _softplus(z)
        lf = -sp if mask is None else jnp.where(mask, -sp, 0.0)
        zs.append(z)
        sps.append(sp)
        pieces += [lf[:, i * s:(i + 1) * s] for i in range(n)]
    lf_rows = jnp.concatenate(pieces, axis=0)
    hi = lf_rows.astype(BF16)
    lo = (lf_rows - hi.astype(F32)).astype(BF16)
    half = lf_rows.shape[0]
    sums = jnp.dot(jnp.concatenate([hi, lo], axis=0), w2, preferred_element_type=F32)
    sums = sums[:half] + sums[half:]
    out = []
    for h in range(len(qs)):
        acc, carry = st[2 * h], st[2 * h + 1]
        afters = []
        for i in range(n):
            blk = sums[(h * n + i) * r:(h * n + i + 1) * r]
            afters.append(blk[:, :s] + carry)
            carry = carry + blk[:, s:]
        after = afters[0] if n == 1 else jnp.concatenate(afters, axis=1)
        w = jnp.exp(zs[h] - sps[h] + after)
        if mask is not None:
            w = jnp.where(mask, w, 0.0)
        acc = acc + jnp.dot(w.astype(BF16), vs[h], preferred_element_type=F32)
        out += [acc, carry]
    return tuple(out)


def _sb_prompt_tile(i, q_ref, k_ref, v_ref, w2_ref, o_ref, tq, tk):
    q = jnp.concatenate([q_ref[:, g * SB_HEAD_DIM:(g + 1) * SB_HEAD_DIM] for g in range(SB_GROUP)], axis=0)
    q = (q * SB_SCALE).astype(BF16)
    r = SB_GROUP * tq
    qpos = i * tq + lax.broadcasted_iota(I32, (r, tk), 0) % tq
    lane = lax.broadcasted_iota(I32, (r, tk), 1)
    w2 = w2_ref[...]
    n_all = (i * tq + tq - 1) // tk + 1
    n_full = (i * tq) // tk

    def block(j, st, masked):
        start = pl.multiple_of(j * tk, tk)
        k = k_ref[pl.ds(start, tk), :].astype(BF16)
        v = v_ref[pl.ds(start, tk), :].astype(BF16)
        mask = ((j * tk + lane) < qpos) if masked else None
        return _sb_chunk([q], [k], [v], w2, mask, st)

    st = (jnp.zeros((r, SB_HEAD_DIM), F32), jnp.zeros((r, tk), F32))
    st = lax.fori_loop(0, n_all - n_full, lambda it, st: block(n_all - 1 - it, st, True), st)
    acc, _ = lax.fori_loop(0, n_full, lambda it, st: block(n_full - 1 - it, st, False), st)
    for g in range(SB_GROUP):
        o_ref[:, g * SB_HEAD_DIM:(g + 1) * SB_HEAD_DIM] = acc[g * tq:(g + 1) * tq].astype(o_ref.dtype)


def _paged_pipeline(pt_ref, n_pages, ppc, copies_for, compute_chunk, state, side_work=None):
    b = pl.program_id(0)
    nc = n_pages // ppc
    total = pl.num_programs(0) * nc
    ahead = RING - 1

    def start_chunk(g, slot):
        for i in range(ppc):
            for k, cp in enumerate(copies_for(pt_ref[g * ppc + i], slot, i)):
                cp.start(priority=(i + k) % N_DMA_THREADS)

    def wait_chunk(slot):
        for i in range(ppc):
            for cp in copies_for(0, slot, i):
                cp.wait()

    @pl.when(b == 0)
    def _():
        for g in range(ahead):
            @pl.when(g < total)
            def _():
                start_chunk(g, g % RING)

    if side_work is not None:
        side_work()

    def body(c, st):
        g = b * nc + c
        slot = g & (RING - 1)

        @pl.when(g + ahead < total)
        def _():
            start_chunk(g + ahead, (g + ahead) & (RING - 1))

        wait_chunk(slot)
        return compute_chunk(slot, st)

    return lax.fori_loop(0, nc, body, state)


def _sb_kernel(pt_ref, q_ref, kn_ref, vn_ref, w2_ref, pq_ref, pk_ref, pv_ref, pw2_ref, ck_hbm, cv_hbm, o_ref, po_ref,
               kbuf, vbuf, sem, *, n_pages, ppc, n_new, nq, tq, tk):
    dh = SB_HEAD_DIM
    rows = n_new * SB_GROUP
    page_rows = SB_KV_HEADS * PAGE_SIZE
    q = (q_ref[0] * SB_SCALE).astype(BF16)
    qs = [q[h] for h in range(SB_KV_HEADS)]
    w2 = w2_ref[...]

    def heads(ref, start, n_keys):
        return [ref[pl.ds(start + h, n_keys, stride=SB_KV_HEADS), :].astype(BF16) for h in range(SB_KV_HEADS)]

    tpos = lax.broadcasted_iota(I32, (rows, PAGE_SIZE), 0) // SB_GROUP
    mask = lax.broadcasted_iota(I32, (rows, PAGE_SIZE), 1) < tpos
    zero = (jnp.zeros((rows, dh), F32), jnp.zeros((rows, PAGE_SIZE), F32))
    st = _sb_chunk(qs, heads(kn_ref.at[0], 0, PAGE_SIZE), heads(vn_ref.at[0], 0, PAGE_SIZE), w2, mask,
                   zero * SB_KV_HEADS)

    def copies_for(page_id, slot, i):
        dst = pl.ds((slot * ppc + i) * page_rows, page_rows)
        return (pltpu.make_async_copy(ck_hbm.at[page_id], kbuf.at[dst], sem.at[0, slot]),
                pltpu.make_async_copy(cv_hbm.at[page_id], vbuf.at[dst], sem.at[1, slot]))

    def compute_chunk(slot, st):
        start = slot * ppc * page_rows
        n_keys = ppc * PAGE_SIZE
        return _sb_chunk(qs, heads(kbuf, start, n_keys), heads(vbuf, start, n_keys), w2, None, st)

    def prompt_tile():
        _sb_prompt_tile(pl.program_id(0) % nq, pq_ref, pk_ref, pv_ref, pw2_ref, po_ref, tq, tk)

    st = _paged_pipeline(pt_ref, n_pages, ppc, copies_for, compute_chunk, st, side_work=prompt_tile)
    for h in range(SB_KV_HEADS):
        o_ref[0, h] = st[2 * h]


def sb_attention(pt_rev, q, k_new, v_new, cache_k, cache_v, ppc, hmat, batch, seq, tk):
    db, kvh, rows, dh = q.shape
    n_pages = pt_rev.shape[0] // db
    page_rows = kvh * PAGE_SIZE
    tq = batch * seq * kvh // db
    nq = seq // tq
    assert tq % 8 == 0 and nq * tq == seq and batch * kvh * nq == db, (batch, seq, db)
    gw = SB_GROUP * dh
    k_col0 = SB_HEADS
    v_col0 = k_col0 + kvh
    p_tile = lambda b, pt: ((b // (kvh * nq)) * nq + b % nq, (b // nq) % kvh)
    grid_spec = pltpu.PrefetchScalarGridSpec(
        num_scalar_prefetch=1, grid=(db,),
        in_specs=[pl.BlockSpec((1, kvh, rows, dh), lambda b, pt: (b, 0, 0, 0)),
                  pl.BlockSpec((1, page_rows, dh), lambda b, pt: (b, 0, 0)),
                  pl.BlockSpec((1, page_rows, dh), lambda b, pt: (b, 0, 0)),
                  pl.BlockSpec((PAGE_SIZE, 2 * PAGE_SIZE), lambda b, pt: (0, 0)),
                  pl.BlockSpec((tq, gw), p_tile),
                  pl.BlockSpec((seq, dh), lambda b, pt: (b // (kvh * nq), k_col0 + (b // nq) % kvh)),
                  pl.BlockSpec((seq, dh), lambda b, pt: (b // (kvh * nq), v_col0 + (b // nq) % kvh)),
                  pl.BlockSpec((tk, 2 * tk), lambda b, pt: (0, 0)),
                  pl.BlockSpec(memory_space=pl.ANY), pl.BlockSpec(memory_space=pl.ANY)],
        out_specs=[pl.BlockSpec((1, kvh, rows, dh), lambda b, pt: (b, 0, 0, 0)), pl.BlockSpec((tq, gw), p_tile)],
        scratch_shapes=[pltpu.VMEM((RING * ppc * page_rows, dh), F32), pltpu.VMEM((RING * ppc * page_rows, dh), F32),
                        pltpu.SemaphoreType.DMA((2, RING))])
    return pl.pallas_call(
        functools.partial(_sb_kernel, n_pages=n_pages, ppc=ppc, n_new=rows // SB_GROUP, nq=nq, tq=tq, tk=tk),
        grid_spec=grid_spec,
        out_shape=[jax.ShapeDtypeStruct(q.shape, F32), jax.ShapeDtypeStruct((batch * seq, SB_HEADS * dh), BF16)],
        compiler_params=_params(("arbitrary",)))(pt_rev, q, k_new, v_new, _tri_sum_matrix(PAGE_SIZE), hmat, hmat, hmat,
                                                 _tri_sum_matrix(tk), cache_k, cache_v)


def _softmax_update(s, v, m, l, acc):
    m_new = jnp.maximum(m, jnp.max(s, axis=-1, keepdims=True))
    alpha = jnp.exp(m - m_new)
    p = jnp.exp(s - m_new)
    l = l * alpha + jnp.sum(p, axis=-1, keepdims=True)
    acc = acc * alpha + jnp.dot(p.astype(BF16), v, preferred_element_type=F32)
    return m_new, l, acc


def _mla_prompt_tile(i, q_ref, k_ref, wuv_ref, o_ref, qs_ref, acc_ref, m_ref, l_ref, tq, tk):
    rows = MLA_HEADS * tq
    for h in range(MLA_HEADS):
        qs_ref[h * tq:(h + 1) * tq, :] = q_ref[:, h * QK_CAT:(h + 1) * QK_CAT]
    acc_ref[...] = jnp.zeros(acc_ref.shape, F32)
    m_ref[...] = jnp.full(m_ref.shape, NEG, F32)
    l_ref[...] = jnp.zeros(l_ref.shape, F32)

    def step(j, masked):
        k = k_ref[pl.ds(pl.multiple_of(j * tk, tk), tk), :]
        s = lax.dot_general(qs_ref[...], k, NT, preferred_element_type=F32)
        if masked:
            qpos = i * tq + lax.broadcasted_iota(I32, (rows, tk), 0) % tq
            s = jnp.where(j * tk + lax.broadcasted_iota(I32, (rows, tk), 1) <= qpos, s, NEG)
        m_prev = m_ref[...]
        m_new = jnp.maximum(m_prev, jnp.max(s, axis=-1, keepdims=True))
        alpha = jnp.exp(m_prev - m_new)
        p = jnp.exp(s - _lanes(m_new, tk))
        l_ref[...] = alpha * l_ref[...] + jnp.sum(p, axis=-1, keepdims=True)
        m_ref[...] = m_new
        acc_ref[...] = acc_ref[...] * _lanes(alpha, KV_LORA) + jnp.dot(p.astype(BF16), k[:, :KV_LORA],
                                                                    preferred_element_type=F32)

    n_full = (i * tq + 1) // tk
    n_all = (i * tq + tq - 1) // tk + 1

    def full_body(j, c):
        step(j, False)
        return c

    def edge_body(j, c):
        step(j, True)
        return c

    lax.fori_loop(0, n_full, full_body, 0)
    lax.fori_loop(n_full, n_all, edge_body, 0)
    for h in range(MLA_HEADS):
        sl = slice(h * tq, (h + 1) * tq)
        lat = (acc_ref[sl, :] / _lanes(l_ref[sl, :], KV_LORA)).astype(BF16)
        o_ref[:, h * V_DIM:(h + 1) * V_DIM] = jnp.dot(lat, wuv_ref[h], preferred_element_type=F32).astype(o_ref.dtype)


def _mla_kernel(pt_ref, q_ref, kn_ref, pq_ref, pk_ref, wuv_ref, c_hbm, r_hbm, o_ref, po_ref, cbuf, rbuf, sem,
                qs_ref, acc_ref, m_ref, l_ref, *, n_pages, ppc, n_new, tq, tk):
    rows = n_new * MLA_HEADS
    q = q_ref[0]
    q_lat, q_rope = q[:, :KV_LORA], q[:, KV_LORA:]
    kn = kn_ref[0]
    tpos = lax.broadcasted_iota(I32, (rows, PAGE_SIZE), 0) // MLA_HEADS
    mask = lax.broadcasted_iota(I32, (rows, PAGE_SIZE), 1) <= tpos
    s_new = jnp.where(mask, lax.dot_general(q, kn, NT, preferred_element_type=F32), NEG)
    init = (jnp.full((rows, 1), NEG, F32), jnp.zeros((rows, 1), F32), jnp.zeros((rows, KV_LORA), F32))
    st = _softmax_update(s_new, kn[:, :KV_LORA], *init)

    def copies_for(page_id, slot, i):
        return (pltpu.make_async_copy(c_hbm.at[page_id], cbuf.at[slot, pl.ds(i * PAGE_SIZE, PAGE_SIZE)], sem.at[0, slot]),
                pltpu.make_async_copy(r_hbm.at[page_id], rbuf.at[slot, i], sem.at[1, slot]))

    n_keys = ppc * PAGE_SIZE
    rope_pad = jnp.zeros((LANES - ROPE, n_keys), BF16)

    def compute_chunk(slot, st):
        c = cbuf[slot].astype(BF16)
        krt = jnp.concatenate([rbuf[slot, i] for i in range(ppc)], axis=1).astype(BF16)
        krt = jnp.concatenate([krt, rope_pad], axis=0)
        s = (lax.dot_general(q_lat, c, NT, preferred_element_type=F32)
             + jnp.dot(q_rope, krt, preferred_element_type=F32))
        return _softmax_update(s, c, *st)

    def prompt_tile():
        nq = pk_ref.shape[0] // tq
        _mla_prompt_tile(pl.program_id(0) % nq, pq_ref, pk_ref, wuv_ref, po_ref, qs_ref, acc_ref, m_ref, l_ref, tq, tk)

    _, l, acc = _paged_pipeline(pt_ref, n_pages, ppc, copies_for, compute_chunk, st, side_work=prompt_tile)
    o_ref[0] = (acc / l).astype(o_ref.dtype)


def mla_attention(pt, q, k_new, cache_ckv, cache_krt, ppc, qcat, kcat, wuv, batch, seq, tk):
    db, rows, _ = q.shape
    n_pages = pt.shape[0] // db
    tq = batch * seq // db
    nq = seq // tq
    assert tq % 16 == 0 and nq * tq == seq and batch * nq == db, (batch, seq, db)
    prow = MLA_HEADS * tq
    p_tile = lambda b, pt: (b, 0)
    grid_spec = pltpu.PrefetchScalarGridSpec(
        num_scalar_prefetch=1, grid=(db,),
        in_specs=[pl.BlockSpec((1, rows, QK_CAT), lambda b, pt: (b, 0, 0)),
                  pl.BlockSpec((1, PAGE_SIZE, QK_CAT), lambda b, pt: (b, 0, 0)),
                  pl.BlockSpec((tq, MLA_HEADS * QK_CAT), p_tile),
                  pl.BlockSpec((seq, QK_CAT), lambda b, pt: (b // nq, 0)),
                  pl.BlockSpec(wuv.shape, lambda b, pt: (0, 0, 0)),
                  pl.BlockSpec(memory_space=pl.ANY), pl.BlockSpec(memory_space=pl.ANY)],
        out_specs=[pl.BlockSpec((1, rows, KV_LORA), lambda b, pt: (b, 0, 0)),
                   pl.BlockSpec((tq, MLA_HEADS * V_DIM), p_tile)],
        scratch_shapes=[pltpu.VMEM((RING, ppc * PAGE_SIZE, KV_LORA), F32), pltpu.VMEM((RING, ppc, ROPE, PAGE_SIZE), F32),
                        pltpu.SemaphoreType.DMA((2, RING)),
                        pltpu.VMEM((prow, QK_CAT), BF16), pltpu.VMEM((prow, KV_LORA), F32),
                        pltpu.VMEM((prow, LANES), F32), pltpu.VMEM((prow, LANES), F32)])
    return pl.pallas_call(
        functools.partial(_mla_kernel, n_pages=n_pages, ppc=ppc, n_new=rows // MLA_HEADS, tq=tq, tk=tk),
        grid_spec=grid_spec,
        out_shape=[jax.ShapeDtypeStruct((db, rows, KV_LORA), BF16),
                   jax.ShapeDtypeStruct((batch * seq, MLA_HEADS * V_DIM), BF16)],
        compiler_params=_params(("arbitrary",)))(pt, q, k_new, qcat, kcat, wuv, cache_ckv, cache_krt)


def _head_mm_kernel(a_ref, w_ref, o_ref):
    o_ref[...] = jnp.dot(a_ref[...], w_ref[0], preferred_element_type=F32).astype(o_ref.dtype)


def head_matmul(a, w, out_dtype):
    m = a.shape[0]
    h, k, n = w.shape
    return pl.pallas_call(
        _head_mm_kernel, grid=(h,),
        in_specs=[pl.BlockSpec((m, k), lambda i: (0, i)), pl.BlockSpec((1, k, n), lambda i: (i, 0, 0))],
        out_specs=pl.BlockSpec((m, n), lambda i: (0, i)),
        out_shape=jax.ShapeDtypeStruct((m, h * n), out_dtype),
        compiler_params=_params(("parallel",)))(a, w)


MEM_PAIR = 2


def _mem_attend(q, k, v):
    s = lax.dot_general((q * MEM_SCALE).astype(BF16), k.astype(BF16), NT, preferred_element_type=F32)
    p = jnp.exp(s - jnp.max(s, axis=-1, keepdims=True))
    o = jnp.dot(p.astype(BF16), v.astype(BF16), preferred_element_type=F32)
    return o / jnp.sum(p, axis=-1, keepdims=True)


def _mem_prompt_kernel(q_ref, k_ref, v_ref, o_ref):
    for h in range(MEM_PAIR):
        sl = slice(h * MEM_DIM, (h + 1) * MEM_DIM)
        o_ref[:, sl] = _mem_attend(q_ref[:, sl], k_ref[:, sl], v_ref[:, sl]).astype(o_ref.dtype)


def mem_attn_prompt(hmat, memkv, q_col0, batch, seq, n_mem, tq):
    nq = seq // tq
    w = MEM_PAIR * MEM_DIM
    n_pair = MEM_HEADS // MEM_PAIR
    return pl.pallas_call(
        _mem_prompt_kernel, grid=(batch, n_pair, nq),
        in_specs=[pl.BlockSpec((tq, w), lambda b, hp, i: (b * nq + i, q_col0 + hp)),
                  pl.BlockSpec((n_mem, w), lambda b, hp, i: (b, hp)),
                  pl.BlockSpec((n_mem, w), lambda b, hp, i: (b, n_pair + hp))],
        out_specs=pl.BlockSpec((tq, w), lambda b, hp, i: (b * nq + i, hp)),
        out_shape=jax.ShapeDtypeStruct((batch * seq, MEM_HEADS * MEM_DIM), BF16),
        compiler_params=_params(("parallel", "parallel", "arbitrary")))(hmat, memkv, memkv)


def _mem_sample_kernel(qa_ref, qb_ref, k_ref, v_ref, o_ref, *, n_new, group, n_mem):
    parts = MEM_DIM // LANES
    stride = MEM_HEADS * parts
    q = jnp.concatenate([qa_ref[...], qb_ref[...]], axis=-1)
    row = lax.broadcasted_iota(I32, (q.shape[0], MEM_DIM), 0) // n_new

    def head(ref, g, h):
        return jnp.concatenate([ref[g, pl.ds(p * MEM_HEADS + h, n_mem, stride=stride), :] for p in range(parts)], axis=-1)

    for h in range(MEM_HEADS):
        qh = q[:, h * MEM_DIM:(h + 1) * MEM_DIM]
        out = jnp.zeros(qh.shape, F32)
        for g in range(group):
            out = jnp.where(row == g, _mem_attend(qh, head(k_ref, g, h), head(v_ref, g, h)), out)
        o_ref[:, h * MEM_DIM:(h + 1) * MEM_DIM] = out.astype(o_ref.dtype)


def mem_sample(hmat, cache_k, cache_v, q_col0, row0, n_new, n_mem):
    db, cache_rows, _ = cache_k.shape
    group = 8 // n_new
    rows = group * n_new
    w = MEM_PAIR * MEM_DIM
    r0 = row0 // rows
    return pl.pallas_call(
        functools.partial(_mem_sample_kernel, n_new=n_new, group=group, n_mem=n_mem), grid=(db // group,),
        in_specs=[pl.BlockSpec((rows, w), lambda i: (r0 + i, q_col0)),
                  pl.BlockSpec((rows, w), lambda i: (r0 + i, q_col0 + 1)),
                  pl.BlockSpec((group, cache_rows, LANES), lambda i: (i, 0, 0)),
                  pl.BlockSpec((group, cache_rows, LANES), lambda i: (i, 0, 0))],
        out_specs=pl.BlockSpec((rows, MEM_HEADS * MEM_DIM), lambda i: (i, 0)),
        out_shape=jax.ShapeDtypeStruct((db * n_new, MEM_HEADS * MEM_DIM), BF16),
        compiler_params=_params(("arbitrary",)))(hmat, hmat, cache_k, cache_v)


def _mix_kernel(osb_ref, omla_ref, omem_ref, wsb_ref, wmla_ref, wmem_ref, gsb_ref, gmla_ref, gmem_ref, o_ref):
    mix = jnp.zeros(o_ref.shape, F32)
    for o, w, g in ((osb_ref, wsb_ref, gsb_ref), (omla_ref, wmla_ref, gmla_ref), (omem_ref, wmem_ref, gmem_ref)):
        mix = mix + jax.nn.sigmoid(g[...]) * jnp.dot(o[...], w[...], preferred_element_type=F32)
    o_ref[...] = mix.astype(o_ref.dtype)


def branch_mix(o_sb, o_mla, o_mem, w_sb, w_mla, w_mem, hmat, gate_col0, d, tm, tn):
    t, kdim = o_sb.shape
    nn = d // tn
    a_spec = pl.BlockSpec((tm, kdim), lambda i, j: (i, 0))
    w_spec = pl.BlockSpec((kdim, tn), lambda i, j: (0, j))
    g_specs = [pl.BlockSpec((tm, tn), lambda i, j, br=br: (i, gate_col0 + br * nn + j)) for br in range(3)]
    return pl.pallas_call(
        _mix_kernel, grid=(t // tm, nn),
        in_specs=[a_spec, a_spec, a_spec, w_spec, w_spec, w_spec] + g_specs,
        out_specs=pl.BlockSpec((tm, tn), lambda i, j: (i, j)),
        out_shape=jax.ShapeDtypeStruct((t, d), BF16),
        compiler_params=_params(("parallel", "arbitrary")))(o_sb, o_mla, o_mem, w_sb, w_mla, w_mem, hmat, hmat, hmat)


def _out_proj_kernel(mix_ref, w_ref, x_ref, g_ref, rhi_ref, rlo_ref, b_ref, xmid_ref, hn_ref, lg_ref):
    xmid = x_ref[...] + jnp.dot(mix_ref[...], w_ref[...], preferred_element_type=F32)
    xmid_ref[...] = xmid
    hn = _rms(xmid, g_ref[...])
    tm, d = hn.shape
    spr = d // LANES
    for c in range(spr):
        hn_ref[pl.ds(c, tm, stride=spr), :] = hn[:, c * LANES:(c + 1) * LANES]
    hi = hn.astype(BF16)
    lo = (hn - hi.astype(F32)).astype(BF16)
    lg = (jnp.dot(hi, rhi_ref[...], preferred_element_type=F32) + jnp.dot(lo, rhi_ref[...], preferred_element_type=F32)
          + jnp.dot(hi, rlo_ref[...], preferred_element_type=F32))
    lg_ref[...] = lg + b_ref[...]


def out_proj(mix, w_out, x, g_ffn, r_hi, r_lo, r_bias, tm):
    t, d = x.shape
    row = lambda i: (i, 0)
    fix = lambda i: (0, 0)
    return pl.pallas_call(
        _out_proj_kernel, grid=(t // tm,),
        in_specs=[pl.BlockSpec((tm, d), row), pl.BlockSpec((d, d), fix), pl.BlockSpec((tm, d), row),
                  pl.BlockSpec((1, d), fix), pl.BlockSpec((d, LANES), fix), pl.BlockSpec((d, LANES), fix),
                  pl.BlockSpec((1, LANES), fix)],
        out_specs=[pl.BlockSpec((tm, d), row), pl.BlockSpec((tm * (d // LANES), LANES), row),
                   pl.BlockSpec((tm, LANES), row)],
        out_shape=[jax.ShapeDtypeStruct((t, d), F32), jax.ShapeDtypeStruct((t * (d // LANES), LANES), F32),
                   jax.ShapeDtypeStruct((t, LANES), F32)],
        compiler_params=_params(("parallel",)))(mix, w_out, x, g_ffn.reshape(1, d), r_hi, r_lo, r_bias)


def _route_kernel(lg_ref, tri_ref, o_ref, cnt_ref, carry_ref):
    @pl.when(pl.program_id(0) == 0)
    def _():
        carry_ref[...] = jnp.zeros(carry_ref.shape, F32)

    lg = lg_ref[...]
    lane = lax.broadcasted_iota(I32, lg.shape, 1)
    lanef = lane.astype(F32)

    def first_max(x):
        v = jnp.max(x, axis=-1, keepdims=True)
        return v, jnp.min(jnp.where(x == v, lanef, 1e6), axis=-1, keepdims=True)

    is_grp = lane < N_GROUPS
    gl = jnp.where(is_grp, lg, NEG)
    gmax, g_idx = first_max(gl)
    g_p = 1.0 / jnp.sum(jnp.where(is_grp, jnp.exp(gl - gmax), 0.0), axis=-1, keepdims=True)
    grp_of_lane = ((lane - N_GROUPS) >> 3).astype(F32)
    el = jnp.where((lane >= N_GROUPS) & (grp_of_lane == g_idx), lg, NEG)
    v1, i1 = first_max(el)
    v2, i2 = first_max(jnp.where(lanef == i1, NEG, el))
    d = jnp.exp(v2 - v1)
    w1 = g_p / (1.0 + d)
    w2 = g_p * d / (1.0 + d)
    e1 = i1 - N_GROUPS
    e2 = i2 - N_GROUPS
    hit1 = lanef == e1
    hit2 = lanef == e2
    both = jnp.where(hit1 | hit2, 1.0, 0.0)
    before = jnp.dot(tri_ref[...], both.astype(BF16), preferred_element_type=F32) + carry_ref[0:1, :]
    r1 = jnp.sum(jnp.where(hit1, before, 0.0), axis=-1, keepdims=True)
    r2 = jnp.sum(jnp.where(hit2, before, 0.0), axis=-1, keepdims=True)
    total = carry_ref[0:1, :] + jnp.sum(both, axis=0, keepdims=True)
    carry_ref[...] = jnp.broadcast_to(total, carry_ref.shape)
    cnt_ref[...] = jnp.broadcast_to(total, cnt_ref.shape)
    o_ref[...] = jnp.where(lane == 0, e1, jnp.where(lane == 1, e2, jnp.where(lane == 2, w1, jnp.where(
        lane == 3, w2, jnp.where(lane == 4, r1, jnp.where(lane == 5, r2, 0.0))))))


def route(logits, tm):
    t = logits.shape[0]
    r = jnp.arange(tm)
    tri = (r[None, :] < r[:, None]).astype(BF16)
    return pl.pallas_call(
        _route_kernel, grid=(t // tm,),
        in_specs=[pl.BlockSpec((tm, LANES), lambda i: (i, 0)), pl.BlockSpec((tm, tm), lambda i: (0, 0))],
        out_specs=[pl.BlockSpec((tm, LANES), lambda i: (i, 0)), pl.BlockSpec((8, LANES), lambda i: (0, 0))],
        out_shape=[jax.ShapeDtypeStruct((t, LANES), F32), jax.ShapeDtypeStruct((8, LANES), F32)],
        scratch_shapes=[pltpu.VMEM((8, LANES), F32)],
        compiler_params=_params(("arbitrary",)))(logits, tri)


def _slab_copies(idx_ref, base, n, src_hbm, dst, dst_row0, spr, sem):
    cps = []
    for r in range(n):
        src_row = pl.multiple_of(idx_ref[base + r] * spr, spr)
        cps.append(pltpu.make_async_copy(src_hbm.at[pl.ds(src_row, spr)], dst.at[pl.ds(dst_row0 + r * spr, spr)], sem))
    return cps


def _expert_kernel(tok_ref, be_ref, first_ref, wslot_ref, nexte_ref, nused_ref, h_hbm, wg_hbm, wu_hbm, wd_hbm, o_ref,
                   xbuf, wg_f32, wu_f32, wd_f32, wg_bf, wu_bf, wd_bf, xsem, wsem, *, spr):
    u = pl.program_id(0)
    n_used = nused_ref[0]
    blk_rows = MOE_BLOCK * spr

    def gather(blk, slot):
        return _slab_copies(tok_ref, blk * MOE_BLOCK, MOE_BLOCK, h_hbm, xbuf, slot * blk_rows, spr, xsem.at[slot])

    def weights(e, slot):
        cps = []
        for src, dst in ((wg_hbm, wg_f32), (wu_hbm, wu_f32), (wd_hbm, wd_f32)):
            rows = src.shape[1] // W_PARTS
            for p in range(W_PARTS):
                band = pl.ds(p * rows, rows)
                cps.append(pltpu.make_async_copy(src.at[e, band], dst.at[slot, band], wsem.at[slot]))
        return cps

    def start_weights(e, slot):
        for k, cp in enumerate(weights(e, slot)):
            cp.start(priority=k % N_DMA_THREADS)

    @pl.when(u == 0)
    def _():
        start_weights(be_ref[0], 0)
        for cp in gather(0, 0):
            cp.start()

    @pl.when(u + 1 < n_used)
    def _():
        for cp in gather(u + 1, (u + 1) & 1):
            cp.start()

    @pl.when(u < n_used)
    def _():
        @pl.when(first_ref[u] == 1)
        def _():
            ws = wslot_ref[u]
            nxt = nexte_ref[u]

            @pl.when(nxt >= 0)
            def _():
                start_weights(nxt, 1 - ws)

            for cp in weights(0, ws):
                cp.wait()
            wg_bf[...] = wg_f32[ws].astype(BF16)
            wu_bf[...] = wu_f32[ws].astype(BF16)
            wd_bf[...] = wd_f32[ws].astype(BF16)

        slot = u & 1
        for cp in gather(u, slot):
            cp.wait()
        x = jnp.concatenate([xbuf[pl.ds(slot * blk_rows + c, MOE_BLOCK, stride=spr), :] for c in range(spr)], axis=-1)
        x = x.astype(BF16)
        gate = jnp.dot(x, wg_bf[...], preferred_element_type=F32)
        up = jnp.dot(x, wu_bf[...], preferred_element_type=F32)
        a = (gate * jax.nn.sigmoid(gate) * up).astype(BF16)
        y = jnp.dot(a, wd_bf[...], preferred_element_type=F32)
        for c in range(spr):
            o_ref[pl.ds(c, MOE_BLOCK, stride=spr), :] = y[:, c * LANES:(c + 1) * LANES]

    @pl.when(u >= n_used)
    def _():
        o_ref[...] = jnp.zeros(o_ref.shape, o_ref.dtype)


def moe_experts(slot_tok, plan, h_slabs, w_gate, w_up, w_down):
    n_slots = slot_tok.shape[0]
    _, d, de = w_gate.shape
    spr = d // LANES
    grid_spec = pltpu.PrefetchScalarGridSpec(
        num_scalar_prefetch=6, grid=(n_slots // MOE_BLOCK,),
        in_specs=[pl.BlockSpec(memory_space=pl.ANY)] * 4,
        out_specs=pl.BlockSpec((MOE_BLOCK * spr, LANES), lambda u, *_: (u, 0)),
        scratch_shapes=[pltpu.VMEM((2 * MOE_BLOCK * spr, LANES), F32),
                        pltpu.VMEM((2, d, de), F32), pltpu.VMEM((2, d, de), F32), pltpu.VMEM((2, de, d), F32),
                        pltpu.VMEM((d, de), BF16), pltpu.VMEM((d, de), BF16), pltpu.VMEM((de, d), BF16),
                        pltpu.SemaphoreType.DMA((2,)), pltpu.SemaphoreType.DMA((2,))])
    return pl.pallas_call(functools.partial(_expert_kernel, spr=spr), grid_spec=grid_spec,
                          out_shape=jax.ShapeDtypeStruct((n_slots * spr, LANES), F32),
                          compiler_params=_params(("arbitrary",)))(slot_tok, *plan, h_slabs, w_gate, w_up, w_down)


def _combine_kernel(s1_ref, s2_ref, x_ref, rt_ref, g_ref, y_hbm, o_ref, ybuf, sem, *, tm, spr):
    i = pl.program_id(0)
    n = pl.num_programs(0)
    half_rows = tm * spr

    def gather(step, slot):
        base = step * tm
        return (_slab_copies(s1_ref, base, tm, y_hbm, ybuf, (2 * slot) * half_rows, spr, sem.at[slot])
                + _slab_copies(s2_ref, base, tm, y_hbm, ybuf, (2 * slot + 1) * half_rows, spr, sem.at[slot]))

    @pl.when(i == 0)
    def _():
        for k, cp in enumerate(gather(0, 0)):
            cp.start(priority=k % N_DMA_THREADS)

    @pl.when(i + 1 < n)
    def _():
        for k, cp in enumerate(gather(i + 1, (i + 1) & 1)):
            cp.start(priority=k % N_DMA_THREADS)

    slot = i & 1
    for cp in gather(i, slot):
        cp.wait()
    rt = rt_ref[...]
    w1, w2 = rt[:, 2:3], rt[:, 3:4]
    ss = jnp.zeros((tm, 1), F32)
    for c in range(spr):
        sl = slice(c * LANES, (c + 1) * LANES)
        y1 = ybuf[pl.ds((2 * slot) * half_rows + c, tm, stride=spr), :]
        y2 = ybuf[pl.ds((2 * slot + 1) * half_rows + c, tm, stride=spr), :]
        xc = x_ref[:, sl] + (w1 * y1 + w2 * y2)
        ss = ss + jnp.sum(xc * xc, axis=-1, keepdims=True)
        o_ref[:, sl] = xc
    scale = lax.rsqrt(ss / (spr * LANES) + NORM_EPS)
    o_ref[...] = o_ref[...] * scale * g_ref[...]


def moe_combine(slot1, slot2, xmid, routed, g_final, y_slabs, tm):
    t, d = xmid.shape
    spr = d // LANES
    grid_spec = pltpu.PrefetchScalarGridSpec(
        num_scalar_prefetch=2, grid=(t // tm,),
        in_specs=[pl.BlockSpec((tm, d), lambda i, a, b: (i, 0)), pl.BlockSpec((tm, LANES), lambda i, a, b: (i, 0)),
                  pl.BlockSpec((1, d), lambda i, a, b: (0, 0)), pl.BlockSpec(memory_space=pl.ANY)],
        out_specs=pl.BlockSpec((tm, d), lambda i, a, b: (i, 0)),
        scratch_shapes=[pltpu.VMEM((4 * tm * spr, LANES), F32), pltpu.SemaphoreType.DMA((2,))])
    return pl.pallas_call(functools.partial(_combine_kernel, tm=tm, spr=spr), grid_spec=grid_spec,
                          out_shape=jax.ShapeDtypeStruct((t, d), F32),
                          compiler_params=_params(("arbitrary",)))(slot1, slot2, xmid, routed, g_final.reshape(1, d), y_slabs)


def _dispatch_plan(e12, rank12, counts):
    t = e12.shape[0]
    n_exp = counts.shape[0]
    tk = 2 * t
    n_blocks = -(-tk // MOE_BLOCK) + n_exp
    padded = (counts + MOE_BLOCK - 1) // MOE_BLOCK * MOE_BLOCK
    pad_end = jnp.cumsum(padded)
    slot_of = (pad_end - padded)[e12] + rank12
    slot_tok = jnp.zeros((n_blocks * MOE_BLOCK,), I32).at[slot_of.reshape(-1)].set(jnp.arange(tk, dtype=I32) // 2)
    blocks = jnp.arange(n_blocks, dtype=I32)
    block_e = jnp.minimum(jnp.searchsorted(pad_end, blocks * MOE_BLOCK, side='right'), n_exp - 1).astype(I32)
    n_used = (pad_end[-1:] // MOE_BLOCK).astype(I32)
    ids = jnp.arange(n_exp, dtype=I32)
    busy = counts > 0
    nxt = lax.cummin(jnp.where(busy, ids, n_exp)[::-1])[::-1]
    next_busy = jnp.concatenate([nxt[1:], jnp.full((1,), n_exp, I32)])
    run = jnp.cumsum(busy.astype(I32)) - 1
    first = ((blocks == ((pad_end - padded) // MOE_BLOCK)[block_e]) & (blocks < n_used[0])).astype(I32)
    wslot = run[block_e] & 1
    next_e = jnp.where(next_busy[block_e] < n_exp, next_busy[block_e], -1).astype(I32)
    return slot_of, slot_tok, (block_e, first, wslot, next_e, n_used)


def _tile(n, target, mult):
    best = None
    for c in range(mult, min(n, target) + 1, mult):
        if n % c == 0:
            best = c
    assert best is not None, (n, target, mult)
    return best


def kernel(x_prompt, x_sample, mem_prompt, cache_sb_k, cache_sb_v, cache_mla_ckv, cache_mla_krope, cache_mem_k, cache_mem_v, page_table, g_attn_norm, w_in, g_q_norm, w_uq, g_kv_norm, w_uk, w_uv, g_mem_norm, w_mem_kv, w_br_sb, w_br_mla, w_br_mem, w_out, g_ffn_norm, w_grp, b_grp, w_rtr, b_rtr, w_gate, w_up, w_down, g_final):
    depth = g_attn_norm.shape[0]
    assert depth == 1
    batch, seq, d = x_prompt.shape
    db, n_new, _ = x_sample.shape
    n_mem = mem_prompt.shape[1]
    n_pages = page_table.shape[1]
    n_pool = cache_sb_k.shape[1]
    tp, ts = batch * seq, db * n_new
    t = tp + ts
    l = 0

    sbq_w, sbkv_w = SB_HEADS * SB_HEAD_DIM, SB_KV_HEADS * SB_HEAD_DIM
    memq_w = MEM_HEADS * MEM_DIM
    c_kr = 2 * sbkv_w + sbq_w + Q_LORA + KV_LORA
    w = w_in[l]
    wp = jnp.concatenate([w[:, :c_kr], w[:, c_kr + ROPE:], w[:, c_kr:c_kr + ROPE], jnp.zeros((d, LANES - ROPE), F32)],
                         axis=1).astype(BF16)
    col_cq = sbq_w + 2 * sbkv_w
    col_ckv = col_cq + Q_LORA
    col_memq = col_ckv + KV_LORA
    col_gate = col_memq + memq_w
    col_kr = col_gate + 3 * d
    n_cols = wp.shape[1]

    half = ROPE // 2
    inv = ROPE_THETA ** (-jnp.arange(half, dtype=F32) / half)
    pos = jnp.concatenate([jnp.tile(jnp.arange(seq), batch), jnp.tile(n_pages * PAGE_SIZE + jnp.arange(n_new), db)])
    ang = pos.astype(F32)[:, None] * inv[None, :]
    cos, sin = jnp.cos(ang), jnp.sin(ang)
    zeros = jnp.zeros_like(cos)
    cos_q, sin_q = jnp.tile(cos, (1, MLA_HEADS)), jnp.tile(sin, (1, MLA_HEADS))
    c_tab = jnp.concatenate([cos, cos, zeros, zeros], axis=1)
    s1_tab = jnp.concatenate([-sin, zeros, zeros, zeros], axis=1)
    s2_tab = jnp.concatenate([zeros, sin, zeros, zeros], axis=1)

    wuq = w_uq[l]
    wuq2 = jnp.concatenate([wuq[:, :, :NOPE].reshape(Q_LORA, -1), wuq[:, :, NOPE:NOPE + half].reshape(Q_LORA, -1),
                            wuq[:, :, NOPE + half:].reshape(Q_LORA, -1)], axis=1).astype(BF16)
    wuk_t = jnp.transpose(w_uk[l], (1, 2, 0)).astype(BF16)
    wuv = jnp.transpose(w_uv[l], (1, 0, 2)).astype(BF16)
    rr = jnp.arange(2 * MLA_HEADS * half)[:, None]
    cc = jnp.arange(MLA_HEADS * LANES)[None, :]
    sel = (cc == (rr % (MLA_HEADS * half)) // half * LANES + rr // (MLA_HEADS * half) * half + rr % half).astype(BF16)

    xcat = jnp.concatenate([x_prompt.reshape(tp, d), x_sample.reshape(ts, d)], axis=0)
    tm_big = _tile(t, 1152, 16)
    hn_attn = rmsnorm(xcat, g_attn_norm[l], BF16, _tile(t, 640, 16))
    hmat = matmul(hn_attn, wp, F32, tm_big, _tile(n_cols, 1024, LANES))

    tm_mid = _tile(t, 640, 16)
    qcat = mla_q(hmat, col_cq // Q_LORA, g_q_norm[l], wuq2, wuk_t, sel, cos_q, sin_q, tm_mid)
    ckv_n, kr_rot, kcat = mla_kv(hmat, col_ckv // KV_LORA, col_kr // LANES, g_kv_norm[l], c_tab, s1_tab, s2_tab, tm_mid)

    memn = rmsnorm(mem_prompt.reshape(batch * n_mem, d), g_mem_norm[l], BF16, _tile(batch * n_mem, 512, 16))
    memkv = matmul(memn, w_mem_kv[l].astype(BF16), F32, _tile(batch * n_mem, 1024, 16), 1024)

    o_mem_p = mem_attn_prompt(hmat, memkv, col_memq // (MEM_PAIR * MEM_DIM), batch, seq, n_mem, _tile(seq, 512, 8))

    hs = hmat[tp:]
    rows_sb = n_new * SB_GROUP
    q_sb = hs[:, :sbq_w].reshape(db, n_new, SB_KV_HEADS, SB_GROUP, SB_HEAD_DIM)
    q_sb = jnp.transpose(q_sb, (0, 2, 1, 3, 4)).reshape(db, SB_KV_HEADS, rows_sb, SB_HEAD_DIM)
    page_rows = PAGE_SIZE * SB_KV_HEADS

    def new_kv(cols):
        x = hs[:, cols:cols + sbkv_w].reshape(db, n_new * SB_KV_HEADS, SB_HEAD_DIM)
        return jnp.pad(x, ((0, 0), (0, page_rows - n_new * SB_KV_HEADS), (0, 0)))

    pt = page_table.astype(I32)
    ppc = _tile(n_pages, 16, 1)
    o_sb_s, o_sb_p = sb_attention(pt[:, ::-1].reshape(-1), q_sb, new_kv(sbq_w), new_kv(sbq_w + sbkv_w),
                                  cache_sb_k[l].reshape(n_pool, page_rows, SB_HEAD_DIM),
                                  cache_sb_v[l].reshape(n_pool, page_rows, SB_HEAD_DIM), ppc, hmat, batch, seq,
                                  min(256, seq))
    o_sb_s = jnp.transpose(o_sb_s.reshape(db, SB_KV_HEADS, n_new, SB_GROUP, SB_HEAD_DIM), (0, 2, 1, 3, 4))
    o_sb_s = o_sb_s.reshape(ts, sbq_w).astype(BF16)

    q_mla_s = qcat[tp:].reshape(db, n_new * MLA_HEADS, QK_CAT)
    kcat_new = jnp.pad(kcat[tp:].reshape(db, n_new, QK_CAT), ((0, 0), (0, PAGE_SIZE - n_new), (0, 0)))
    lat_s, o_mla_p = mla_attention(pt.reshape(-1), q_mla_s, kcat_new, cache_mla_ckv[l],
                                   jnp.transpose(cache_mla_krope[l], (0, 2, 1)), ppc, qcat, kcat, wuv, batch, seq,
                                   min(512, seq))
    o_mla_s = head_matmul(lat_s.reshape(ts, MLA_HEADS * KV_LORA), wuv, BF16)

    def mem_rows(c):
        c = c.reshape(db, n_mem, MEM_HEADS, MEM_DIM // LANES, LANES)
        return jnp.transpose(c, (0, 1, 3, 2, 4)).reshape(db, n_mem * memq_w // LANES, LANES)

    o_mem_s = mem_sample(hmat, mem_rows(cache_mem_k[l]), mem_rows(cache_mem_v[l]),
                         col_memq // (MEM_PAIR * MEM_DIM), tp, n_new, n_mem)

    o_sb = jnp.concatenate([o_sb_p, o_sb_s], axis=0)
    o_mla = jnp.concatenate([o_mla_p, o_mla_s], axis=0)
    o_mem = jnp.concatenate([o_mem_p, o_mem_s], axis=0)
    tn_mix = 512
    mix = branch_mix(o_sb, o_mla, o_mem, w_br_sb[l].astype(BF16), w_br_mla[l].astype(BF16), w_br_mem[l].astype(BF16),
                     hmat, col_gate // tn_mix, d, tm_big, tn_mix)
    w_r = jnp.concatenate([w_grp[l], w_rtr[l], jnp.zeros((d, LANES - N_GROUPS - N_EXPERTS), F32)], axis=1)
    r_hi = w_r.astype(BF16)
    r_lo = (w_r - r_hi.astype(F32)).astype(BF16)
    r_bias = jnp.concatenate([b_grp[l], b_rtr[l], jnp.zeros((LANES - N_GROUPS - N_EXPERTS,), F32)]).reshape(1, LANES)
    xmid, hn_ffn, logits = out_proj(mix, w_out[l].astype(BF16), xcat, g_ffn_norm[l], r_hi, r_lo, r_bias, _tile(t, 320, 16))

    routed, pair_counts = route(logits, _tile(t, 1152, 8))
    slot_of, slot_tok, plan = _dispatch_plan(routed[:, 0:2].astype(I32), routed[:, 4:6].astype(I32),
                                             pair_counts[0, :N_EXPERTS].astype(I32))
    y_slabs = moe_experts(slot_tok, plan, hn_ffn, w_gate[l], w_up[l], w_down[l])
    y = moe_combine(slot_of[:, 0], slot_of[:, 1], xmid, routed, g_final, y_slabs, _tile(t, 128, 8))

    def kv_out(cols, rows, shape):
        return rows[:, cols[0]:cols[1]].reshape(shape)[None]

    hp = hmat[:tp]
    return (y[:tp].reshape(batch, seq, d), y[tp:].reshape(db, n_new, d),
            kv_out((sbq_w, sbq_w + sbkv_w), hp, (batch, seq, SB_KV_HEADS, SB_HEAD_DIM)),
            kv_out((sbq_w + sbkv_w, sbq_w + 2 * sbkv_w), hp, (batch, seq, SB_KV_HEADS, SB_HEAD_DIM)),
            ckv_n[:tp].reshape(batch, seq, KV_LORA)[None], kr_rot[:tp].reshape(batch, seq, ROPE)[None],
            memkv[:, :memq_w].reshape(batch, n_mem, MEM_HEADS, MEM_DIM)[None],
            memkv[:, memq_w:].reshape(batch, n_mem, MEM_HEADS, MEM_DIM)[None],
            kv_out((sbq_w, sbq_w + sbkv_w), hs, (db, n_new, SB_KV_HEADS, SB_HEAD_DIM)),
            kv_out((sbq_w + sbkv_w, sbq_w + 2 * sbkv_w), hs, (db, n_new, SB_KV_HEADS, SB_HEAD_DIM)),
            ckv_n[tp:].reshape(db, n_new, KV_LORA)[None], kr_rot[tp:].reshape(db, n_new, ROPE)[None])
```
